```python
import jax, jax.numpy as jnp
from jax import lax
import numpy as np

D_MODEL = 4096
BATCH = 4
SEQ = 2048
DEPTH = 2
DEC_BATCH = 128
DEC_SEQ = 8
PAST_LEN = 16384
PAGE_SIZE = 128

D_RNN = D_MODEL
N_GATE_BLOCKS = 16
GATE_BLOCK = D_RNN // N_GATE_BLOCKS
CONV_WIDTH = 4
LRU_C = 8.0
POOL_WINDOWS = (2, 4, 8, 16)
N_POOL_GROUPS = len(POOL_WINDOWS)
POOL_GROUP = D_MODEL // N_POOL_GROUPS
POOL_BUF = max(POOL_WINDOWS) - 1
D_FF = 7 * D_MODEL // 2
N_EXPERTS = 8
TOP_K = 2
N_A = (DEPTH + 1) // 2
N_B = DEPTH // 2
ALPHA = (2 * DEPTH) ** 0.25
BETA = (8 * DEPTH) ** -0.25
LN_EPS = 1e-5

kernel_name = "hawk_pool_moe_hybrid_step"


def layer_norm(x, g, b):
    xf = x.astype(jnp.float32)
    mu = jnp.mean(xf, axis=-1, keepdims=True)
    var = jnp.mean(jnp.square(xf - mu), axis=-1, keepdims=True)
    return ((xf - mu) * lax.rsqrt(var + LN_EPS) * g.astype(jnp.float32) + b.astype(jnp.float32)).astype(x.dtype)


def rglru_block(x, h0, conv_prev, w_in, conv_w, conv_b, w_a, b_a, w_x, b_x, lam, w_out):
    B, T, _ = x.shape
    u = jnp.einsum('btd,de->bte', x, w_in)
    gate_branch = jax.nn.gelu(u[..., :D_RNN], approximate=True)
    rnn_in = u[..., D_RNN:]
    xp = jnp.concatenate([conv_prev.astype(rnn_in.dtype), rnn_in], axis=1)
    xc = conv_b + sum(xp[:, k:k + T] * conv_w[k] for k in range(CONV_WIDTH))
    new_conv = xp[:, -(CONV_WIDTH - 1):]
    xb = xc.reshape(B, T, N_GATE_BLOCKS, GATE_BLOCK)
    r = jax.nn.sigmoid(jnp.einsum('bthi,hij->bthj', xb, w_a).reshape(B, T, D_RNN) + b_a)
    ig = jax.nn.sigmoid(jnp.einsum('bthi,hij->bthj', xb, w_x).reshape(B, T, D_RNN) + b_x)
    log_a = (-LRU_C * r.astype(jnp.float32)) * jax.nn.softplus(-lam.astype(jnp.float32))
    a = jnp.exp(log_a)
    b = jnp.sqrt(-jnp.expm1(2.0 * log_a)) * (ig * xc).astype(jnp.float32)

    def step(h, ab):
        a_t, b_t = ab
        h = a_t * h + b_t
        return h, h

    h_last, hs = lax.scan(step, h0.astype(jnp.float32), (jnp.swapaxes(a, 0, 1), jnp.swapaxes(b, 0, 1)))
    y = jnp.swapaxes(hs, 0, 1).astype(x.dtype)
    out = jnp.einsum('bte,ed->btd', y * gate_branch, w_out)
    return out, h_last.astype(x.dtype), new_conv


def pool_mixer(x, prev, pos0, w_grp, b_grp, scale):
    B, T, D = x.shape
    xp = jnp.concatenate([prev.astype(x.dtype), x], axis=1)
    cs = jnp.cumsum(xp.astype(jnp.float32), axis=1)
    cs = jnp.concatenate([jnp.zeros((B, 1, D), jnp.float32), cs], axis=1)
    end = cs[:, POOL_BUF + 1:]
    pos = pos0 + jnp.arange(T)
    parts = []
    for g, w in enumerate(POOL_WINDOWS):
        sl = slice(g * POOL_GROUP, (g + 1) * POOL_GROUP)
        start = cs[:, POOL_BUF + 1 - w:POOL_BUF + 1 - w + T, sl]
        cnt = jnp.minimum(w, pos + 1).astype(jnp.float32)[None, :, None]
        parts.append((end[..., sl] - start) / cnt)
    pooled = jnp.concatenate(parts, axis=-1)
    mixed = (pooled - x.astype(jnp.float32)).astype(x.dtype).reshape(B, T, N_POOL_GROUPS, POOL_GROUP)
    out = jnp.einsum('btgi,gij->btgj', mixed, w_grp).reshape(B, T, D) + b_grp
    return out * scale, xp[:, -POOL_BUF:]


def swiglu(x, wg, wu, wd):
    hidden = jax.nn.silu(jnp.einsum('btd,df->btf', x, wg)) * jnp.einsum('btd,df->btf', x, wu)
    return jnp.einsum('btf,fd->btd', hidden, wd)


def moe_swiglu(x, w_router, wg, wu, wd):
    logits = jnp.einsum('btd,de->bte', x, w_router).astype(jnp.float32)
    top_v, top_i = lax.top_k(logits, TOP_K)
    top_w = jax.nn.softmax(top_v, axis=-1)
    combine = jnp.sum(jax.nn.one_hot(top_i, N_EXPERTS, dtype=jnp.float32) * top_w[..., None], axis=-2)
    out = jnp.zeros(x.shape, jnp.float32)
    for e in range(N_EXPERTS):
        out = out + combine[..., e:e + 1] * swiglu(x, wg[e], wu[e], wd[e]).astype(jnp.float32)
    return out.astype(x.dtype)


def setup_inputs(seed: int = 0) -> dict:
    key = jax.random.key(seed)
    ks = jax.random.split(key, 32)
    f32 = jnp.float32

    def nrm(k, shape, scale):
        return jax.random.normal(k, shape, f32) * scale

    u = jax.random.uniform(ks[12], (N_A, D_RNN), f32, 0.9, 0.999)
    s = u ** (1.0 / LRU_C)
    rg_lambda = jnp.log(s) - jnp.log1p(-s)
    return {
        'x_prompt': nrm(ks[0], (BATCH, SEQ, D_MODEL), 1.0),
        'x_sample': nrm(ks[1], (DEC_BATCH, DEC_SEQ, D_MODEL), 1.0),
        'state_rglru_h': nrm(ks[2], (N_A, DEC_BATCH, D_RNN), 0.5),
        'state_rglru_conv': nrm(ks[3], (N_A, DEC_BATCH, CONV_WIDTH - 1, D_RNN), 1.0),
        'state_pool': nrm(ks[4], (N_B, DEC_BATCH, POOL_BUF, D_MODEL), 1.0),
        'rg_w_in': nrm(ks[5], (N_A, D_MODEL, 2 * D_RNN), D_MODEL ** -0.5),
        'rg_conv_w': nrm(ks[6], (N_A, CONV_WIDTH, D_RNN), CONV_WIDTH ** -0.5),
        'rg_conv_b': nrm(ks[7], (N_A, D_RNN), 0.01),
        'rg_w_a': nrm(ks[8], (N_A, N_GATE_BLOCKS, GATE_BLOCK, GATE_BLOCK), GATE_BLOCK ** -0.5),
        'rg_b_a': nrm(ks[9], (N_A, D_RNN), 0.01),
        'rg_w_x': nrm(ks[10], (N_A, N_GATE_BLOCKS, GATE_BLOCK, GATE_BLOCK), GATE_BLOCK ** -0.5),
        'rg_b_x': nrm(ks[11], (N_A, D_RNN), 0.01),
        'rg_lambda': rg_lambda,
        'rg_w_out': nrm(ks[13], (N_A, D_RNN, D_MODEL), BETA * D_RNN ** -0.5),
        'pool_w': nrm(ks[14], (N_B, N_POOL_GROUPS, POOL_GROUP, POOL_GROUP), BETA * POOL_GROUP ** -0.5),
        'pool_b': nrm(ks[15], (N_B, D_MODEL), 0.01),
        'pool_scale': 1.0 + nrm(ks[16], (N_B, D_MODEL), 0.02),
        'ffn_w_gate': nrm(ks[17], (N_A, D_MODEL, D_FF), D_MODEL ** -0.5),
        'ffn_w_up': nrm(ks[18], (N_A, D_MODEL, D_FF), D_MODEL ** -0.5),
        'ffn_w_down': nrm(ks[19], (N_A, D_FF, D_MODEL), BETA * D_FF ** -0.5),
        'moe_router': nrm(ks[20], (N_B, D_MODEL, N_EXPERTS), D_MODEL ** -0.5),
        'moe_w_gate': nrm(ks[21], (N_B, N_EXPERTS, D_MODEL, D_FF), D_MODEL ** -0.5),
        'moe_w_up': nrm(ks[22], (N_B, N_EXPERTS, D_MODEL, D_FF), D_MODEL ** -0.5),
        'moe_w_down': nrm(ks[23], (N_B, N_EXPERTS, D_FF, D_MODEL), BETA * D_FF ** -0.5),
        'ln_mix_g': 1.0 + nrm(ks[24], (DEPTH, D_MODEL), 0.02),
        'ln_mix_b': nrm(ks[25], (DEPTH, D_MODEL), 0.02),
        'ln_ffn_g': 1.0 + nrm(ks[26], (DEPTH, D_MODEL), 0.02),
        'ln_ffn_b': nrm(ks[27], (DEPTH, D_MODEL), 0.02),
    }


def reference(x_prompt, x_sample, state_rglru_h, state_rglru_conv, state_pool,
              rg_w_in, rg_conv_w, rg_conv_b, rg_w_a, rg_b_a, rg_w_x, rg_b_x, rg_lambda, rg_w_out,
              pool_w, pool_b, pool_scale, ffn_w_gate, ffn_w_up, ffn_w_down,
              moe_router, moe_w_gate, moe_w_up, moe_w_down,
              ln_mix_g, ln_mix_b, ln_ffn_g, ln_ffn_b):

    def trunk(x, h0, conv0, pool0, pos0):
        new_h, new_conv, new_pool = [], [], []
        for i in range(DEPTH):
            j = i // 2
            if i % 2 == 0:
                mix, h, c = rglru_block(x, h0[j], conv0[j], rg_w_in[j], rg_conv_w[j], rg_conv_b[j],
                                        rg_w_a[j], rg_b_a[j], rg_w_x[j], rg_b_x[j], rg_lambda[j], rg_w_out[j])
                new_h.append(h)
                new_conv.append(c)
            else:
                mix, p = pool_mixer(x, pool0[j], pos0, pool_w[j], pool_b[j], pool_scale[j])
                new_pool.append(p)
            x = layer_norm(ALPHA * x + mix, ln_mix_g[i], ln_mix_b[i])
            if i % 2 == 0:
                ffn = swiglu(x, ffn_w_gate[j], ffn_w_up[j], ffn_w_down[j])
            else:
                ffn = moe_swiglu(x, moe_router[j], moe_w_gate[j], moe_w_up[j], moe_w_down[j])
            x = layer_norm(ALPHA * x + ffn, ln_ffn_g[i], ln_ffn_b[i])
        return x, jnp.stack(new_h), jnp.stack(new_conv), jnp.stack(new_pool)

    dt = x_prompt.dtype
    h0_p = jnp.zeros((N_A, BATCH, D_RNN), dt)
    conv0_p = jnp.zeros((N_A, BATCH, CONV_WIDTH - 1, D_RNN), dt)
    pool0_p = jnp.zeros((N_B, BATCH, POOL_BUF, D_MODEL), dt)
    y_prompt, h_p, conv_p, pool_p = trunk(x_prompt, h0_p, conv0_p, pool0_p, 0)
    y_sample, h_s, conv_s, pool_s = trunk(x_sample, state_rglru_h, state_rglru_conv, state_pool, PAST_LEN)
    return (y_prompt, y_sample, h_p, conv_p, pool_p, h_s, conv_s, pool_s)
```

```python
import functools

import jax
import jax.numpy as jnp
from jax import lax
from jax.experimental import pallas as pl
from jax.experimental.pallas import tpu as pltpu

F32 = jnp.float32
BF16 = jnp.bfloat16

LRU_C = 8.0
LN_EPS = 1e-5
PAST_LEN = 16384
TOP_K = 2
GELU_C = 0.7978845608028654

LANES = 128
SUBLANES = 8
MXU_DIM = 256
SUB_ROWS = 256
GATHER_ROWS = 256
VMEM_LIMIT = 56 * 1024 * 1024


def _pick(n, target, mult):
    best = None
    d = mult
    while d <= min(n, target):
        if n % d == 0:
            best = d
        d += mult
    return best if best is not None else n


def _cp(sem):
    return pltpu.CompilerParams(dimension_semantics=sem, vmem_limit_bytes=VMEM_LIMIT)


def _dot(a, b):
    return jnp.dot(a, b, preferred_element_type=F32)


def _gelu_tanh(x):
    return 0.5 * x * (1.0 + jnp.tanh(GELU_C * (x + 0.044715 * (x * x * x))))


def _softplus(z):
    return jnp.log1p(jnp.exp(-jnp.abs(z))) + jnp.maximum(z, 0.0)


def _layer_norm_rows(z, g, b):
    mu = jnp.mean(z, axis=-1, keepdims=True)
    zc = z - mu
    var = jnp.mean(zc * zc, axis=-1, keepdims=True)
    return zc * lax.rsqrt(var + LN_EPS) * g + b


def _mm_fullk_kernel(x_ref, *refs, kind, n_w, n_gelu_tiles):
    w_refs = refs[:n_w]
    out_ref = refs[n_w]
    wbf = refs[n_w + 1:]
    n = pl.program_id(0)
    m = pl.program_id(1)

    @pl.when(m == 0)
    def _():
        for w_ref, s in zip(w_refs, wbf):
            s[...] = w_ref[...].astype(BF16)

    x = x_ref[...]
    if kind == "swiglu":
        g = _dot(x, wbf[0][...])
        u = _dot(x, wbf[1][...])
        out_ref[...] = (g * jax.nn.sigmoid(g) * u).astype(out_ref.dtype)
    elif kind == "gelu_split":
        u = _dot(x, wbf[0][...])

        @pl.when(n < n_gelu_tiles)
        def _():
            out_ref[...] = _gelu_tanh(u)

        @pl.when(n >= n_gelu_tiles)
        def _():
            out_ref[...] = u
    else:
        out_ref[...] = _dot(x, wbf[0][...]).astype(out_ref.dtype)


def _mm_fullk(x, ws, *, kind, out_dtype, tm, tn, n_gelu_cols=0, name):
    M, K = x.shape
    N = ws[0].shape[1]
    tm = _pick(M, tm, 16)
    tn = _pick(N, tn, LANES)
    n_w = len(ws)
    kern = functools.partial(_mm_fullk_kernel, kind=kind, n_w=n_w, n_gelu_tiles=n_gelu_cols // tn)
    return pl.pallas_call(
        kern,
        grid=(N // tn, M // tm),
        in_specs=[pl.BlockSpec((tm, K), lambda n, m: (m, 0))]
        + [pl.BlockSpec((K, tn), lambda n, m: (0, n)) for _ in ws],
        out_specs=pl.BlockSpec((tm, tn), lambda n, m: (m, n)),
        out_shape=jax.ShapeDtypeStruct((M, N), out_dtype),
        scratch_shapes=[pltpu.VMEM((K, tn), BF16) for _ in ws],
        compiler_params=_cp(("arbitrary", "arbitrary")),
        name=name,
    )(x, *ws)


def _mm_ktiled_kernel(x_ref, w_ref, o_ref):
    k = pl.program_id(2)
    part = _dot(x_ref[...], w_ref[...].astype(BF16))

    @pl.when(k == 0)
    def _():
        o_ref[...] = part

    @pl.when(k > 0)
    def _():
        o_ref[...] += part


def _mm_ktiled(x, w, *, tm, tn, tk, name):
    M, K = x.shape
    N = w.shape[1]
    tm = _pick(M, tm, 16)
    tn = _pick(N, tn, LANES)
    tk = _pick(K, tk, LANES)
    return pl.pallas_call(
        _mm_ktiled_kernel,
        grid=(M // tm, N // tn, K // tk),
        in_specs=[pl.BlockSpec((tm, tk), lambda m, n, k: (m, k)),
                  pl.BlockSpec((tk, tn), lambda m, n, k: (k, n))],
        out_specs=pl.BlockSpec((tm, tn), lambda m, n, k: (m, n)),
        out_shape=jax.ShapeDtypeStruct((M, N), F32),
        compiler_params=_cp(("arbitrary", "arbitrary", "arbitrary")),
        name=name,
    )(x, w)


def _ln_kernel(x_ref, y_ref, g_ref, b_ref, of_ref, ob_ref, *, alpha):
    z = alpha * x_ref[...] + y_ref[...]
    o = _layer_norm_rows(z, g_ref[...], b_ref[...])
    of_ref[...] = o
    ob_ref[...] = o.astype(BF16)


def _deepnorm(x, y, g, b, *, alpha, name):
    M, D = x.shape
    tm = _pick(M, 256, 16)
    row = pl.BlockSpec((tm, D), lambda i: (i, 0))
    vec = pl.BlockSpec((1, D), lambda i: (0, 0))
    return pl.pallas_call(
        functools.partial(_ln_kernel, alpha=alpha),
        grid=(M // tm,),
        in_specs=[row, row, vec, vec],
        out_specs=[row, row],
        out_shape=[jax.ShapeDtypeStruct((M, D), F32), jax.ShapeDtypeStruct((M, D), BF16)],
        compiler_params=_cp(("arbitrary",)),
        name=name,
    )(x, y, g.reshape(1, D), b.reshape(1, D))


def _rglru_coeffs(xc, wax_ref, ba, bx, lam, gb):
    xcb = xc.astype(BF16)
    nblk = xc.shape[1] // gb
    r_parts, i_parts = [], []
    for j in range(nblk):
        ru = _dot(xcb[:, j * gb:(j + 1) * gb], wax_ref[j])
        r_parts.append(ru[:, :gb])
        i_parts.append(ru[:, gb:])
    r = jax.nn.sigmoid(jnp.concatenate(r_parts, axis=1) + ba)
    ig = jax.nn.sigmoid(jnp.concatenate(i_parts, axis=1) + bx)
    log_a = (-LRU_C * r) * _softplus(-lam)
    a = jnp.exp(log_a)
    t = jnp.tanh(log_a)
    one_minus_a2 = (-2.0 * t) / (1.0 - t)
    b = jnp.sqrt(one_minus_a2) * (ig * xc)
    return a, b


def _rglru_prompt_kernel(gate_ref, rnn_ref, cw_ref, cb_ref, wax_ref, ba_ref, bx_ref, lam_ref,
                         yg_ref, hlast_ref, xp_scr, h_scr, a_scr, b_scr, *, tc, gb, cw):
    b_idx = pl.program_id(1)
    i = pl.program_id(2)
    nt = pl.num_programs(2)
    halo = SUBLANES

    @pl.when(i == 0)
    def _():
        xp_scr[0:halo, :] = jnp.zeros((halo, xp_scr.shape[1]), F32)
        h_scr[...] = jnp.zeros(h_scr.shape, F32)

    xp_scr[halo:halo + tc, :] = rnn_ref[...]
    xc = cb_ref[...]
    for k in range(cw):
        xc = xc + xp_scr[pl.ds(halo - (cw - 1) + k, tc), :] * cw_ref[k:k + 1, :]
    a, b = _rglru_coeffs(xc, wax_ref, ba_ref[...], bx_ref[...], lam_ref[...], gb)
    a_scr[...] = a
    b_scr[...] = b

    def step(t, h):
        h = a_scr[pl.ds(t, 1), :] * h + b_scr[pl.ds(t, 1), :]
        b_scr[pl.ds(t, 1), :] = h
        return h

    h = lax.fori_loop(0, tc, step, h_scr[0:1, :], unroll=8)
    h_scr[0:1, :] = h
    yg_ref[...] = (b_scr[...] * gate_ref[...]).astype(BF16)
    xp_scr[0:halo, :] = xp_scr[tc:tc + halo, :]

    @pl.when(i == nt - 1)
    def _():
        hlast_ref[pl.ds(b_idx, 1), :] = h


def _rglru_prompt(u, cwt, cbias, wax, ba, bx, lam, *, batch, seq, d_rnn, gb):
    C = _pick(d_rnn, 1024, gb)
    tc = _pick(seq, 256, SUBLANES)
    nt = seq // tc
    nc = d_rnn // C
    cw = cwt.shape[0]
    vec = pl.BlockSpec((1, C), lambda c, b, i: (0, c))
    kern = functools.partial(_rglru_prompt_kernel, tc=tc, gb=gb, cw=cw)
    return pl.pallas_call(
        kern,
        grid=(nc, batch, nt),
        in_specs=[pl.BlockSpec((tc, C), lambda c, b, i: (b * nt + i, c)),
                  pl.BlockSpec((tc, C), lambda c, b, i: (b * nt + i, nc + c)),
                  pl.BlockSpec((cw, C), lambda c, b, i: (0, c)),
                  vec,
                  pl.BlockSpec((C // gb, gb, 2 * gb), lambda c, b, i: (c, 0, 0)),
                  vec, vec, vec],
        out_specs=[pl.BlockSpec((tc, C), lambda c, b, i: (b * nt + i, c)),
                   pl.BlockSpec((batch, C), lambda c, b, i: (0, c))],
        out_shape=[jax.ShapeDtypeStruct((batch * seq, d_rnn), BF16),
                   jax.ShapeDtypeStruct((batch, d_rnn), F32)],
        scratch_shapes=[pltpu.VMEM((tc + SUBLANES, C), F32), pltpu.VMEM((SUBLANES, C), F32),
                        pltpu.VMEM((tc, C), F32), pltpu.VMEM((tc, C), F32)],
        compiler_params=_cp(("arbitrary", "arbitrary", "arbitrary")),
        name="rglru_prompt",
    )(u, u, cwt, cbias, wax, ba, bx, lam)


def _rglru_sample_kernel(gate_ref, rnn_ref, cs_ref, h0_ref, cw_ref, cb_ref, wax_ref, ba_ref, bx_ref,
                         lam_ref, yg_ref, hlast_ref, xc_scr, *, steps, bsz, gb, cw):
    slabs = [cs_ref[k] for k in range(cw - 1)] + [rnn_ref[t] for t in range(steps)]
    for t in range(steps):
        xc = cb_ref[...]
        for k in range(cw):
            xc = xc + slabs[t + k] * cw_ref[k:k + 1, :]
        xc_scr[t * bsz:(t + 1) * bsz, :] = xc
    a, b = _rglru_coeffs(xc_scr[...], wax_ref, ba_ref[...], bx_ref[...], lam_ref[...], gb)
    h = h0_ref[...]
    for t in range(steps):
        h = a[t * bsz:(t + 1) * bsz, :] * h + b[t * bsz:(t + 1) * bsz, :]
        yg_ref[t] = (h * gate_ref[t]).astype(BF16)
    hlast_ref[...] = h


def _rglru_sample(gate3, rnn3, cs3, h0, cwt, cbias, wax, ba, bx, lam, *, gb):
    steps, bsz, d_rnn = rnn3.shape
    C = _pick(d_rnn, 1024, gb)
    cw = cwt.shape[0]
    vec = pl.BlockSpec((1, C), lambda c: (0, c))
    kern = functools.partial(_rglru_sample_kernel, steps=steps, bsz=bsz, gb=gb, cw=cw)
    return pl.pallas_call(
        kern,
        grid=(d_rnn // C,),
        in_specs=[pl.BlockSpec((steps, bsz, C), lambda c: (0, 0, c)),
                  pl.BlockSpec((steps, bsz, C), lambda c: (0, 0, c)),
                  pl.BlockSpec((cw - 1, bsz, C), lambda c: (0, 0, c)),
                  pl.BlockSpec((bsz, C), lambda c: (0, c)),
                  pl.BlockSpec((cw, C), lambda c: (0, c)),
                  vec,
                  pl.BlockSpec((C // gb, gb, 2 * gb), lambda c: (c, 0, 0)),
                  vec, vec, vec],
        out_specs=[pl.BlockSpec((steps, bsz, C), lambda c: (0, 0, c)),
                   pl.BlockSpec((bsz, C), lambda c: (0, c))],
        out_shape=[jax.ShapeDtypeStruct((steps, bsz, d_rnn), BF16),
                   jax.ShapeDtypeStruct((bsz, d_rnn), F32)],
        scratch_shapes=[pltpu.VMEM((steps * bsz, C), F32)],
        compiler_params=_cp(("arbitrary",)),
        name="rglru_sample",
    )(gate3, rnn3, cs3, h0, cwt, cbias, wax, ba, bx, lam)


def _pool_prompt_kernel(x_ref, pw_ref, pb_ref, ps_ref, g_ref, b_ref, of_ref, ob_ref, xp_scr, z_scr,
                        *, tm, windows, alpha):
    i = pl.program_id(1)
    halo = 2 * SUBLANES
    D = x_ref.shape[1]
    pg = D // len(windows)

    @pl.when(i == 0)
    def _():
        xp_scr[0:halo, :] = jnp.zeros((halo, D), F32)

    xp_scr[halo:halo + tm, :] = x_ref[...]
    pos = i * tm + lax.broadcasted_iota(jnp.int32, (tm, 1), 0)
    for gi, w in enumerate(windows):
        cols = slice(gi * pg, (gi + 1) * pg)
        xg = x_ref[:, cols]
        s = xg
        for j in range(1, w):
            s = s + xp_scr[pl.ds(halo - j, tm), cols]
        cnt = jnp.minimum(w, pos + 1).astype(F32)
        mixed = (s / cnt - xg).astype(BF16)
        out = (_dot(mixed, pw_ref[gi]) + pb_ref[:, cols]) * ps_ref[:, cols]
        z_scr[:, cols] = alpha * xg + out
    o = _layer_norm_rows(z_scr[...], g_ref[...], b_ref[...])
    of_ref[...] = o
    ob_ref[...] = o.astype(BF16)
    xp_scr[0:halo, :] = xp_scr[tm:tm + halo, :]


def _pool_prompt(x, pw, pb, ps, g, b, *, batch, seq, windows, alpha):
    D = x.shape[1]
    tm = _pick(seq, 256, 16)
    nt = seq // tm
    ng = len(windows)
    row = pl.BlockSpec((tm, D), lambda bb, i: (bb * nt + i, 0))
    vec = pl.BlockSpec((1, D), lambda bb, i: (0, 0))
    kern = functools.partial(_pool_prompt_kernel, tm=tm, windows=windows, alpha=alpha)
    return pl.pallas_call(
        kern,
        grid=(batch, nt),
        in_specs=[row, pl.BlockSpec((ng, D // ng, D // ng), lambda bb, i: (0, 0, 0)), vec, vec, vec, vec],
        out_specs=[row, row],
        out_shape=[jax.ShapeDtypeStruct((batch * seq, D), F32),
                   jax.ShapeDtypeStruct((batch * seq, D), BF16)],
        scratch_shapes=[pltpu.VMEM((tm + 2 * SUBLANES, D), F32), pltpu.VMEM((tm, D), F32)],
        compiler_params=_cp(("arbitrary", "arbitrary")),
        name="pool_prompt",
    )(x, pw, pb, ps, g, b)


def _pool_sample_kernel(st_ref, x_ref, pw_ref, pb_ref, ps_ref, g_ref, b_ref, of_ref, ob_ref,
                        mix_scr, z_scr, *, steps, bt, windows, pos0, alpha):
    nbuf = st_ref.shape[0]
    D = x_ref.shape[2]
    pg = D // len(windows)

    def slab(p, cols):
        return st_ref[p, :, cols] if p < nbuf else x_ref[p - nbuf, :, cols]

    for gi, w in enumerate(windows):
        cols = slice(gi * pg, (gi + 1) * pg)
        for t in range(steps):
            xg = x_ref[t, :, cols]
            s = xg
            for j in range(1, w):
                s = s + slab(nbuf + t - j, cols)
            cnt = float(min(w, pos0 + t + 1))
            mix_scr[t * bt:(t + 1) * bt, cols] = (s / cnt - xg).astype(BF16)
            z_scr[t * bt:(t + 1) * bt, cols] = alpha * xg
    for gi in range(len(windows)):
        cols = slice(gi * pg, (gi + 1) * pg)
        out = (_dot(mix_scr[:, cols], pw_ref[gi]) + pb_ref[:, cols]) * ps_ref[:, cols]
        z_scr[:, cols] = z_scr[:, cols] + out
    o = _layer_norm_rows(z_scr[...], g_ref[...], b_ref[...])
    for t in range(steps):
        of_ref[t] = o[t * bt:(t + 1) * bt, :]
        ob_ref[t] = o[t * bt:(t + 1) * bt, :].astype(BF16)


def _pool_sample(st3, x3, pw, pb, ps, g, b, *, windows, pos0, alpha):
    nbuf, bsz, D = st3.shape
    steps = x3.shape[0]
    bt = _pick(bsz, 16, 16)
    ng = len(windows)
    vec = pl.BlockSpec((1, D), lambda j: (0, 0))
    blk = pl.BlockSpec((steps, bt, D), lambda j: (0, j, 0))
    kern = functools.partial(_pool_sample_kernel, steps=steps, bt=bt, windows=windows, pos0=pos0, alpha=alpha)
    return pl.pallas_call(
        kern,
        grid=(bsz // bt,),
        in_specs=[pl.BlockSpec((nbuf, bt, D), lambda j: (0, j, 0)), blk,
                  pl.BlockSpec((ng, D // ng, D // ng), lambda j: (0, 0, 0)), vec, vec, vec, vec],
        out_specs=[blk, blk],
        out_shape=[jax.ShapeDtypeStruct((steps, bsz, D), F32), jax.ShapeDtypeStruct((steps, bsz, D), BF16)],
        scratch_shapes=[pltpu.VMEM((steps * bt, D), BF16), pltpu.VMEM((steps * bt, D), F32)],
        compiler_params=_cp(("arbitrary",)),
        name="pool_sample",
    )(st3, x3, pw, pb, ps, g, b)


def _router_kernel(x_ref, wr_ref, sel_ref, e12_ref, w12_ref, *, n_exp):
    x = x_ref[...]
    xh = x.astype(BF16)
    xl = (x - xh.astype(F32)).astype(BF16)
    w = wr_ref[...]
    wh = w.astype(BF16)
    wl = (w - wh.astype(F32)).astype(BF16)
    logits = _dot(xh, wh) + (_dot(xh, wl) + _dot(xl, wh))
    lane = lax.broadcasted_iota(jnp.int32, logits.shape, 1)
    neg = jnp.float32(-jnp.inf)
    logits = jnp.where(lane < n_exp, logits, neg)
    m1 = jnp.max(logits, axis=-1, keepdims=True)
    i1 = jnp.min(jnp.where(logits == m1, lane, LANES), axis=-1, keepdims=True)
    rest = jnp.where(lane == i1, neg, logits)
    m2 = jnp.max(rest, axis=-1, keepdims=True)
    i2 = jnp.min(jnp.where(rest == m2, lane, LANES), axis=-1, keepdims=True)
    e = jnp.exp(m2 - m1)
    den = 1.0 + e
    w1 = 1.0 / den
    w2 = e / den
    sel_ref[...] = jnp.where((lane == i1) | (lane == i2), 1.0, 0.0).astype(F32)
    e12_ref[...] = jnp.where(lane == 0, i1, jnp.where(lane == 1, i2, 0))
    w12_ref[...] = jnp.where(lane == 0, w1, jnp.where(lane == 1, w2, 0.0))


def _router(x, wr_pad, *, n_exp):
    M, D = x.shape
    tm = _pick(M, 512, 16)
    row = pl.BlockSpec((tm, LANES), lambda i: (i, 0))
    return pl.pallas_call(
        functools.partial(_router_kernel, n_exp=n_exp),
        grid=(M // tm,),
        in_specs=[pl.BlockSpec((tm, D), lambda i: (i, 0)), pl.BlockSpec((D, LANES), lambda i: (0, 0))],
        out_specs=[row, row, row],
        out_shape=[jax.ShapeDtypeStruct((M, LANES), F32), jax.ShapeDtypeStruct((M, LANES), jnp.int32),
                   jax.ShapeDtypeStruct((M, LANES), F32)],
        compiler_params=_cp(("arbitrary",)),
        name="moe_router",
    )(x, wr_pad)


def _rank_kernel(sel_ref, rank_ref, cnt_ref, carry):
    i = pl.program_id(0)

    @pl.when(i == 0)
    def _():
        carry[...] = jnp.zeros(carry.shape, F32)

    sel = sel_ref[...]
    tm = sel.shape[0]
    r = lax.broadcasted_iota(jnp.int32, (tm, tm), 0)
    c = lax.broadcasted_iota(jnp.int32, (tm, tm), 1)
    tri = jnp.where(c < r, 1.0, 0.0).astype(BF16)
    excl = _dot(tri, sel.astype(BF16)) + carry[0:1, :]
    rank_ref[...] = excl
    tot = excl[tm - 1:tm, :] + sel[tm - 1:tm, :]
    carry[0:1, :] = tot
    cnt_ref[...] = jnp.broadcast_to(tot, cnt_ref.shape)


def _rank(sel):
    M = sel.shape[0]
    tm = _pick(M, 256, 16)
    return pl.pallas_call(
        _rank_kernel,
        grid=(M // tm,),
        in_specs=[pl.BlockSpec((tm, LANES), lambda i: (i, 0))],
        out_specs=[pl.BlockSpec((tm, LANES), lambda i: (i, 0)), pl.BlockSpec((SUBLANES, LANES), lambda i: (0, 0))],
        out_shape=[jax.ShapeDtypeStruct((M, LANES), F32), jax.ShapeDtypeStruct((SUBLANES, LANES), F32)],
        scratch_shapes=[pltpu.VMEM((SUBLANES, LANES), F32)],
        compiler_params=_cp(("arbitrary",)),
        name="moe_rank",
    )(sel)


def _slot_kernel(rank_ref, e12_ref, base_ref, pos_ref):
    lane = lax.broadcasted_iota(jnp.int32, rank_ref.shape, 1)
    slot = rank_ref[...] + base_ref[...]
    e12 = e12_ref[...]
    p1 = jnp.sum(jnp.where(lane == e12[:, 0:1], slot, 0.0), axis=-1, keepdims=True)
    p2 = jnp.sum(jnp.where(lane == e12[:, 1:2], slot, 0.0), axis=-1, keepdims=True)
    pos_ref[...] = jnp.where(lane == 0, p1, jnp.where(lane == 1, p2, 0.0)).astype(jnp.int32)


def _slots(rank, e12, base_row):
    M = rank.shape[0]
    tm = _pick(M, 512, 16)
    row = pl.BlockSpec((tm, LANES), lambda i: (i, 0))
    return pl.pallas_call(
        _slot_kernel,
        grid=(M // tm,),
        in_specs=[row, row, pl.BlockSpec((1, LANES), lambda i: (0, 0))],
        out_specs=row,
        out_shape=jax.ShapeDtypeStruct((M, LANES), jnp.int32),
        compiler_params=_cp(("arbitrary",)),
        name="moe_slots",
    )(rank, e12, base_row)


def _invert_kernel(p1_ref, p2_ref, src_ref, *, n_tok, n_slot):
    def zero(s, c):
        src_ref[s] = 0
        return c

    lax.fori_loop(0, n_slot, zero, 0, unroll=8)

    def put(t, c):
        src_ref[p1_ref[t]] = t
        src_ref[p2_ref[t]] = t
        return c

    lax.fori_loop(0, n_tok, put, 0, unroll=8)


def _invert(p1, p2, n_slot):
    n_tok = p1.shape[0]
    smem = pl.BlockSpec(memory_space=pltpu.SMEM)
    return pl.pallas_call(
        functools.partial(_invert_kernel, n_tok=n_tok, n_slot=n_slot),
        in_specs=[smem, smem],
        out_specs=smem,
        out_shape=jax.ShapeDtypeStruct((n_slot,), jnp.int32),
        name="moe_invert",
    )(p1, p2)


def _row_copy(src_hbm, row, dst, i, sem):
    return pltpu.make_async_copy(src_hbm.at[pl.ds(row, 1), :], dst.at[pl.ds(i, 1), :], sem)


def _dispatch_kernel(valid_ref, src_ref, x_hbm, out_ref, buf, sem, *, rows):
    q = pl.program_id(0)

    @pl.when(valid_ref[q] > 0)
    def _():
        def issue(i, c):
            _row_copy(x_hbm, src_ref[0, 0, i], buf, i, sem).start()
            return c

        lax.fori_loop(0, rows, issue, 0, unroll=8)

        def drain(i, c):
            _row_copy(x_hbm, 0, buf, i, sem).wait()
            return c

        lax.fori_loop(0, rows, drain, 0, unroll=8)
        out_ref[...] = buf[...].astype(BF16)

    @pl.when(valid_ref[q] == 0)
    def _():
        out_ref[...] = jnp.zeros(out_ref.shape, BF16)


def _dispatch(x, src, tile_valid):
    n_slot = src.shape[0]
    D = x.shape[1]
    rows = GATHER_ROWS
    nt = n_slot // rows
    return pl.pallas_call(
        functools.partial(_dispatch_kernel, rows=rows),
        grid_spec=pltpu.PrefetchScalarGridSpec(
            num_scalar_prefetch=1,
            grid=(nt,),
            in_specs=[pl.BlockSpec((1, 1, rows), lambda q, v: (q, 0, 0), memory_space=pltpu.SMEM),
                      pl.BlockSpec(memory_space=pl.ANY)],
            out_specs=pl.BlockSpec((rows, D), lambda q, v: (q, 0)),
            scratch_shapes=[pltpu.VMEM((rows, D), F32), pltpu.SemaphoreType.DMA],
        ),
        out_shape=jax.ShapeDtypeStruct((n_slot, D), BF16),
        compiler_params=_cp(("arbitrary",)),
        name="moe_dispatch",
    )(tile_valid, src.reshape(nt, 1, rows), x)


def _combine_kernel(p1_ref, p2_ref, x_ref, w12_ref, g_ref, b_ref, y_hbm, o_ref, y1, y2, sem1, sem2,
                    *, rows, alpha):
    def issue(i, c):
        _row_copy(y_hbm, p1_ref[0, 0, i], y1, i, sem1).start()
        _row_copy(y_hbm, p2_ref[0, 0, i], y2, i, sem2).start()
        return c

    lax.fori_loop(0, rows, issue, 0, unroll=8)

    def drain(i, c):
        _row_copy(y_hbm, 0, y1, i, sem1).wait()
        _row_copy(y_hbm, 0, y2, i, sem2).wait()
        return c

    lax.fori_loop(0, rows, drain, 0, unroll=8)
    w12 = w12_ref[...]
    ffn = w12[:, 0:1] * y1[...] + w12[:, 1:2] * y2[...]
    o_ref[...] = _layer_norm_rows(alpha * x_ref[...] + ffn, g_ref[...], b_ref[...])


def _combine(x, y_sorted, pos1, pos2, w12, g, b, *, alpha):
    M, D = x.shape
    rows = _pick(M, GATHER_ROWS, 16)
    nt = M // rows
    idx = pl.BlockSpec((1, 1, rows), lambda i: (i, 0, 0), memory_space=pltpu.SMEM)
    row = pl.BlockSpec((rows, D), lambda i: (i, 0))
    vec = pl.BlockSpec((1, D), lambda i: (0, 0))
    return pl.pallas_call(
        functools.partial(_combine_kernel, rows=rows, alpha=alpha),
        grid=(nt,),
        in_specs=[idx, idx, row, pl.BlockSpec((rows, LANES), lambda i: (i, 0)), vec, vec,
                  pl.BlockSpec(memory_space=pl.ANY)],
        out_specs=row,
        out_shape=jax.ShapeDtypeStruct((M, D), F32),
        scratch_shapes=[pltpu.VMEM((rows, D), F32), pltpu.VMEM((rows, D), F32),
                        pltpu.SemaphoreType.DMA, pltpu.SemaphoreType.DMA],
        compiler_params=_cp(("arbitrary",)),
        name="moe_combine",
    )(pos1.reshape(nt, 1, rows), pos2.reshape(nt, 1, rows), x, w12, g.reshape(1, D), b.reshape(1, D), y_sorted)


def _moe_up_kernel(exp_ref, nsub_ref, csrc_ref, x_ref, wg_ref, wu_ref, h_ref, wgb, wub, *, n_sub):
    c = pl.program_id(0)
    nsub = nsub_ref[c]

    @pl.when(nsub > 0)
    def _():
        wgb[...] = wg_ref[...].astype(BF16)
        wub[...] = wu_ref[...].astype(BF16)

        def body(s, carry):
            r0 = pl.multiple_of(s * SUB_ROWS, SUB_ROWS)
            x = x_ref[pl.ds(r0, SUB_ROWS), :]
            g = _dot(x, wgb[...])
            u = _dot(x, wub[...])
            h_ref[pl.ds(r0, SUB_ROWS), :] = (g * jax.nn.sigmoid(g) * u).astype(BF16)
            return carry

        lax.fori_loop(0, nsub, body, 0)

        def fill(s, carry):
            r0 = pl.multiple_of(s * SUB_ROWS, SUB_ROWS)
            h_ref[pl.ds(r0, SUB_ROWS), :] = jnp.zeros((SUB_ROWS, h_ref.shape[1]), BF16)
            return carry

        lax.fori_loop(nsub, n_sub, fill, 0)


def _moe_up(xs, wg, wu, ch_exp, ch_nsub, ch_src, *, tm_chunk, tf):
    n_slot, D = xs.shape
    F = wg.shape[2]
    tf = _pick(F, tf, LANES)
    nf = F // tf
    nch = n_slot // tm_chunk

    def fidx(c, f, nsub):
        return jnp.where(nsub[c] > 0, f, nf - 1)

    return pl.pallas_call(
        functools.partial(_moe_up_kernel, n_sub=tm_chunk // SUB_ROWS),
        grid_spec=pltpu.PrefetchScalarGridSpec(
            num_scalar_prefetch=3,
            grid=(nch, nf),
            in_specs=[pl.BlockSpec((tm_chunk, D), lambda c, f, e, ns, cs: (cs[c], 0),
                                   pipeline_mode=pl.Buffered(1)),
                      pl.BlockSpec((None, D, tf), lambda c, f, e, ns, cs: (e[c], 0, fidx(c, f, ns))),
                      pl.BlockSpec((None, D, tf), lambda c, f, e, ns, cs: (e[c], 0, fidx(c, f, ns)))],
            out_specs=pl.BlockSpec((tm_chunk, tf), lambda c, f, e, ns, cs: (cs[c], fidx(c, f, ns))),
            scratch_shapes=[pltpu.VMEM((D, tf), BF16), pltpu.VMEM((D, tf), BF16)],
        ),
        out_shape=jax.ShapeDtypeStruct((n_slot, F), BF16),
        compiler_params=_cp(("arbitrary", "arbitrary")),
        name="moe_up",
    )(ch_exp, ch_nsub, ch_src, xs, wg, wu)


def _moe_down_kernel(exp_ref, nsub_ref, csrc_ref, h_ref, wd_ref, y_ref, wdb, *, n_sub):
    c = pl.program_id(0)
    k = pl.program_id(2)
    nsub = nsub_ref[c]

    @pl.when(nsub > 0)
    def _():
        wdb[...] = wd_ref[...].astype(BF16)

        @pl.when(k == 0)
        def _():
            def first(s, carry):
                r0 = pl.multiple_of(s * SUB_ROWS, SUB_ROWS)
                y_ref[pl.ds(r0, SUB_ROWS), :] = _dot(h_ref[pl.ds(r0, SUB_ROWS), :], wdb[...])
                return carry

            lax.fori_loop(0, nsub, first, 0)

            def fill(s, carry):
                r0 = pl.multiple_of(s * SUB_ROWS, SUB_ROWS)
                y_ref[pl.ds(r0, SUB_ROWS), :] = jnp.zeros((SUB_ROWS, y_ref.shape[1]), F32)
                return carry

            lax.fori_loop(nsub, n_sub, fill, 0)

        @pl.when(k > 0)
        def _():
            def acc(s, carry):
                r0 = pl.multiple_of(s * SUB_ROWS, SUB_ROWS)
                y_ref[pl.ds(r0, SUB_ROWS), :] += _dot(h_ref[pl.ds(r0, SUB_ROWS), :], wdb[...])
                return carry

            lax.fori_loop(0, nsub, acc, 0)


def _moe_down(hs, wd, ch_exp, ch_nsub, ch_src, *, tm_chunk, tn, tk):
    n_slot, F = hs.shape
    D = wd.shape[2]
    tn = _pick(D, tn, LANES)
    tk = _pick(F, tk, LANES)
    nn, nk = D // tn, F // tk
    nch = n_slot // tm_chunk

    def nidx(c, n, ns):
        return jnp.where(ns[c] > 0, n, nn - 1)

    def kidx(c, k, ns):
        return jnp.where(ns[c] > 0, k, nk - 1)

    return pl.pallas_call(
        functools.partial(_moe_down_kernel, n_sub=tm_chunk // SUB_ROWS),
        grid_spec=pltpu.PrefetchScalarGridSpec(
            num_scalar_prefetch=3,
            grid=(nch, nn, nk),
            in_specs=[pl.BlockSpec((tm_chunk, tk), lambda c, n, k, e, ns, cs: (cs[c], kidx(c, k, ns))),
                      pl.BlockSpec((None, tk, tn),
                                   lambda c, n, k, e, ns, cs: (e[c], kidx(c, k, ns), nidx(c, n, ns)))],
            out_specs=pl.BlockSpec((tm_chunk, tn), lambda c, n, k, e, ns, cs: (cs[c], nidx(c, n, ns))),
            scratch_shapes=[pltpu.VMEM((tk, tn), BF16)],
        ),
        out_shape=jax.ShapeDtypeStruct((n_slot, D), F32),
        compiler_params=_cp(("arbitrary", "arbitrary", "arbitrary")),
        name="moe_down",
    )(ch_exp, ch_nsub, ch_src, hs, wd)


def _chunk_tables(cnt, *, n_exp, tm_chunk, n_chunk):
    cnt = cnt.astype(jnp.int32)
    nch_e = (cnt + tm_chunk - 1) // tm_chunk
    cend = jnp.cumsum(nch_e)
    cstart = cend - nch_e
    total = cend[-1]
    j = jnp.arange(n_chunk, dtype=jnp.int32)
    jj = jnp.minimum(j, total - 1)
    e = jnp.minimum(jnp.sum((jj[:, None] >= cend[None, :]).astype(jnp.int32), axis=1), n_exp - 1)
    local = jj - cstart[e]
    rows = jnp.clip(cnt[e] - local * tm_chunk, 0, tm_chunk)
    nsub = jnp.where(j < total, (rows + SUB_ROWS - 1) // SUB_ROWS, 0).astype(jnp.int32)
    base = (cstart * tm_chunk).astype(F32)
    return e.astype(jnp.int32), nsub, jj.astype(jnp.int32), base


def _moe_ffn(x_f32, w_router, wg, wu, wd, ln_g, ln_b, *, alpha):
    M, D = x_f32.shape
    n_exp = wg.shape[0]
    avg = TOP_K * M / n_exp
    tm_chunk = SUB_ROWS * max(1, -(-int(avg * 1.11) // SUB_ROWS))
    n_chunk = n_exp + (TOP_K * M) // tm_chunk
    n_slot = n_chunk * tm_chunk

    wr_pad = jnp.pad(w_router, ((0, 0), (0, LANES - n_exp)))
    sel, e12, w12 = _router(x_f32, wr_pad, n_exp=n_exp)
    rank, cnt = _rank(sel)
    ch_exp, ch_nsub, ch_src, base = _chunk_tables(cnt[0, :n_exp], n_exp=n_exp, tm_chunk=tm_chunk,
                                                  n_chunk=n_chunk)
    base_row = jnp.pad(base, (0, LANES - n_exp)).reshape(1, LANES)
    pos = _slots(rank, e12, base_row)
    pos1, pos2 = pos[:, 0], pos[:, 1]
    src = _invert(pos1, pos2, n_slot)
    sub_per_chunk = tm_chunk // SUB_ROWS
    tile_valid = (jnp.arange(sub_per_chunk, dtype=jnp.int32)[None, :] < ch_nsub[:, None])
    tile_valid = tile_valid.astype(jnp.int32).reshape(-1)
    if GATHER_ROWS != SUB_ROWS:
        raise ValueError("dispatch tiles must match the MoE sub-tile rows")
    xs = _dispatch(x_f32, src, tile_valid)
    hs = _moe_up(xs, wg, wu, ch_exp, ch_nsub, ch_src, tm_chunk=tm_chunk, tf=256)
    ys = _moe_down(hs, wd, ch_exp, ch_nsub, ch_src, tm_chunk=tm_chunk, tn=1024, tk=1024)
    return _combine(x_f32, ys, pos1, pos2, w12, ln_g, ln_b, alpha=alpha)


def kernel(x_prompt, x_sample, state_rglru_h, state_rglru_conv, state_pool, rg_w_in, rg_conv_w, rg_conv_b,
           rg_w_a, rg_b_a, rg_w_x, rg_b_x, rg_lambda, rg_w_out, pool_w, pool_b, pool_scale, ffn_w_gate,
           ffn_w_up, ffn_w_down, moe_router, moe_w_gate, moe_w_up, moe_w_down, ln_mix_g, ln_mix_b,
           ln_ffn_g, ln_ffn_b):
    B, S, D = x_prompt.shape
    Bs, Ss, _ = x_sample.shape
    depth = ln_mix_g.shape[0]
    alpha = float((2 * depth) ** 0.25)
    d_rnn = rg_w_a.shape[1] * rg_w_a.shape[2]
    gb = rg_w_a.shape[2]
    n_pool_groups = pool_w.shape[1]
    windows = tuple(2 ** (i + 1) for i in range(n_pool_groups))
    pool_buf = state_pool.shape[2]
    conv_w = rg_conv_w.shape[1]
    Mp, Ms = B * S, Bs * Ss

    xs_tm = jnp.swapaxes(x_sample, 0, 1)
    x = jnp.concatenate([x_prompt.reshape(Mp, D), xs_tm.reshape(Ms, D)], axis=0)
    xb = x.astype(BF16)

    new_h_p, new_conv_p, new_pool_p = [], [], []
    new_h_s, new_conv_s, new_pool_s = [], [], []
    for i in range(depth):
        j = i // 2
        if i % 2 == 0:
            u = _mm_fullk(xb, [rg_w_in[j]], kind="gelu_split", out_dtype=F32, tm=1024, tn=512,
                          n_gelu_cols=d_rnn, name="rg_in_proj")
            wax = jnp.concatenate([rg_w_a[j], rg_w_x[j]], axis=-1).astype(BF16)
            vec = lambda v: v.reshape(1, d_rnn)
            args = (rg_conv_w[j], vec(rg_conv_b[j]), wax, vec(rg_b_a[j]), vec(rg_b_x[j]), vec(rg_lambda[j]))
            yg_p, h_p = _rglru_prompt(u, *args, batch=B, seq=S, d_rnn=d_rnn, gb=gb)
            u_s = u[Mp:].reshape(Ss, Bs, 2 * d_rnn)
            cs3 = jnp.swapaxes(state_rglru_conv[j], 0, 1)
            yg_s, h_s = _rglru_sample(u_s[:, :, :d_rnn], u_s[:, :, d_rnn:], cs3, state_rglru_h[j], *args, gb=gb)
            new_h_p.append(h_p)
            new_h_s.append(h_s)
            rnn_p = u[:Mp, d_rnn:].reshape(B, S, d_rnn)
            new_conv_p.append(rnn_p[:, S - (conv_w - 1):])
            xp_s = jnp.concatenate([cs3, u_s[:, :, d_rnn:]], axis=0)
            new_conv_s.append(jnp.swapaxes(xp_s[-(conv_w - 1):], 0, 1))
            yg = jnp.concatenate([yg_p, yg_s.reshape(Ms, d_rnn)], axis=0)
            mix = _mm_fullk(yg, [rg_w_out[j]], kind="plain", out_dtype=F32, tm=1024, tn=512, name="rg_out_proj")
            x, xb = _deepnorm(x, mix, ln_mix_g[i], ln_mix_b[i], alpha=alpha, name="ln_mix")
            hid = _mm_fullk(xb, [ffn_w_gate[j], ffn_w_up[j]], kind="swiglu", out_dtype=BF16, tm=1024, tn=256,
                            name="ffn_up")
            ffn = _mm_ktiled(hid, ffn_w_down[j], tm=2304, tn=1024, tk=1024, name="ffn_down")
            x, xb = _deepnorm(x, ffn, ln_ffn_g[i], ln_ffn_b[i], alpha=alpha, name="ln_ffn")
        else:
            pw = pool_w[j].astype(BF16)
            vec = lambda v: v.reshape(1, D)
            pargs = (pw, vec(pool_b[j]), vec(pool_scale[j]), vec(ln_mix_g[i]), vec(ln_mix_b[i]))
            x_s3 = x[Mp:].reshape(Ss, Bs, D)
            st3 = jnp.swapaxes(state_pool[j], 0, 1)
            xp_f, _ = _pool_prompt(x, *pargs, batch=B, seq=S, windows=windows, alpha=alpha)
            xs_f, _ = _pool_sample(st3, x_s3, *pargs, windows=windows, pos0=PAST_LEN, alpha=alpha)
            new_pool_p.append(x[:Mp].reshape(B, S, D)[:, S - pool_buf:])
            hist = jnp.concatenate([st3, x_s3], axis=0)
            new_pool_s.append(jnp.swapaxes(hist[-pool_buf:], 0, 1))
            x = jnp.concatenate([xp_f, xs_f.reshape(Ms, D)], axis=0)
            x = _moe_ffn(x, moe_router[j], moe_w_gate[j], moe_w_up[j], moe_w_down[j],
                         ln_ffn_g[i], ln_ffn_b[i], alpha=alpha)
            xb = x.astype(BF16)

    y_prompt = x[:Mp].reshape(B, S, D)
    y_sample = jnp.swapaxes(x[Mp:].reshape(Ss, Bs, D), 0, 1)
    return (y_prompt, y_sample, jnp.stack(new_h_p), jnp.stack(new_conv_p), jnp.stack(new_pool_p),
            jnp.stack(new_h_s), jnp.stack(new_conv_s), jnp.stack(new_pool_s))
```

```python
import functools
import math

import jax
import jax.numpy as jnp
from jax import lax
from jax.experimental import pallas as pl
from jax.experimental.pallas import tpu as pltpu

F32 = jnp.float32
BF16 = jnp.bfloat16

LRU_C = 8.0
LN_EPS = 1e-5
PAST_LEN = 16384
TOP_K = 2
GELU_C = 0.7978845608028654

LANES = 128
SUBLANES = 8
SUB_ROWS = 256
BIG_SUBS = 4
VMEM_LIMIT = 56 * 1024 * 1024


def _pick(n, target, mult):
    best = None
    d = mult
    while d <= min(n, target):
        if n % d == 0:
            best = d
        d += mult
    return best if best is not None else n


def _cp(sem):
    return pltpu.CompilerParams(dimension_semantics=sem, vmem_limit_bytes=VMEM_LIMIT)


def _dot(a, b):
    return jnp.dot(a, b, preferred_element_type=F32)


def _gelu_tanh(x):
    return 0.5 * x * (1.0 + jnp.tanh(GELU_C * (x + 0.044715 * (x * x * x))))


def _softplus(z):
    return jnp.log1p(jnp.exp(-jnp.abs(z))) + jnp.maximum(z, 0.0)


def _sigmoid_tanh(x):
    return 0.5 + 0.5 * jnp.tanh(0.5 * x)


def _layer_norm_rows(z, g, b):
    mu = jnp.mean(z, axis=-1, keepdims=True)
    zc = z - mu
    var = jnp.mean(zc * zc, axis=-1, keepdims=True)
    return zc * lax.rsqrt(var + LN_EPS) * g + b


def _row_sources(arrs, tm, row_axis):
    specs, ranges = [], []
    lo = 0
    for a in arrs:
        nb = a.shape[0] // tm
        if nb * tm != a.shape[0]:
            raise ValueError("row source not divisible by the row tile")

        def imap(*idx, lo=lo, nb=nb):
            return (jnp.clip(idx[row_axis] - lo, 0, nb - 1), 0)

        specs.append(pl.BlockSpec((tm, a.shape[1]), imap))
        ranges.append((lo, lo + nb))
        lo += nb
    return specs, ranges, lo


def _for_row_source(m, ranges, refs, body):
    if len(refs) == 1:
        body(refs[0])
        return
    for (lo, hi), r in zip(ranges, refs):
        @pl.when((m >= lo) & (m < hi))
        def _(r=r):
            body(r)


def _common_tile(arrs, target, mult):
    return _pick(math.gcd(*[a.shape[0] for a in arrs]), target, mult)


def _mm_fullk_kernel(*refs, kind, n_x, n_w, ranges, n_gelu_tiles):
    x_refs = refs[:n_x]
    w_refs = refs[n_x:n_x + n_w]
    out_ref = refs[n_x + n_w]
    wbf = refs[n_x + n_w + 1:]
    n = pl.program_id(0)
    m = pl.program_id(1)

    @pl.when(m == 0)
    def _():
        for w_ref, s in zip(w_refs, wbf):
            s[...] = w_ref[...].astype(BF16)

    def body(x_ref):
        x = x_ref[...]
        if kind == "swiglu":
            g = _dot(x, wbf[0][...])
            u = _dot(x, wbf[1][...])
            out_ref[...] = (g * jax.nn.sigmoid(g) * u).astype(out_ref.dtype)
        elif kind == "gelu_split":
            u = _dot(x, wbf[0][...])

            @pl.when(n < n_gelu_tiles)
            def _():
                out_ref[...] = _gelu_tanh(u)

            @pl.when(n >= n_gelu_tiles)
            def _():
                out_ref[...] = u
        else:
            out_ref[...] = _dot(x, wbf[0][...]).astype(out_ref.dtype)

    _for_row_source(m, ranges, x_refs, body)


def _mm_fullk(xs, ws, *, kind, out_dtype, tm, tn, n_gelu_cols=0, name):
    K = xs[0].shape[1]
    N = ws[0].shape[1]
    tm = _common_tile(xs, tm, 16)
    tn = _pick(N, tn, LANES)
    x_specs, ranges, n_m = _row_sources(xs, tm, row_axis=1)
    kern = functools.partial(_mm_fullk_kernel, kind=kind, n_x=len(xs), n_w=len(ws), ranges=ranges,
                             n_gelu_tiles=n_gelu_cols // tn)
    return pl.pallas_call(
        kern,
        grid=(N // tn, n_m),
        in_specs=x_specs + [pl.BlockSpec((K, tn), lambda n, m: (0, n)) for _ in ws],
        out_specs=pl.BlockSpec((tm, tn), lambda n, m: (m, n)),
        out_shape=jax.ShapeDtypeStruct((n_m * tm, N), out_dtype),
        scratch_shapes=[pltpu.VMEM((K, tn), BF16) for _ in ws],
        compiler_params=_cp(("arbitrary", "arbitrary")),
        name=name,
    )(*xs, *ws)


def _mm_ktiled_kernel(x_ref, w_ref, o_ref, wbf, *, n_piece):
    k = pl.program_id(2)
    wbf[...] = w_ref[...].astype(BF16)
    rows = x_ref.shape[0] // n_piece

    @pl.when(k == 0)
    def _():
        for p in range(n_piece):
            o_ref[p * rows:(p + 1) * rows, :] = _dot(x_ref[p * rows:(p + 1) * rows, :], wbf[...])

    @pl.when(k > 0)
    def _():
        for p in range(n_piece):
            o_ref[p * rows:(p + 1) * rows, :] += _dot(x_ref[p * rows:(p + 1) * rows, :], wbf[...])


def _mm_ktiled(x, w, *, tm, tn, tk, name):
    M, K = x.shape
    N = w.shape[1]
    tm = _pick(M, tm, 16)
    tn = _pick(N, tn, LANES)
    tk = _pick(K, tk, LANES)
    n_piece = 3 if tm % (3 * 16) == 0 else 1
    return pl.pallas_call(
        functools.partial(_mm_ktiled_kernel, n_piece=n_piece),
        grid=(M // tm, N // tn, K // tk),
        in_specs=[pl.BlockSpec((tm, tk), lambda m, n, k: (m, k)),
                  pl.BlockSpec((tk, tn), lambda m, n, k: (k, n))],
        out_specs=pl.BlockSpec((tm, tn), lambda m, n, k: (m, n)),
        out_shape=jax.ShapeDtypeStruct((M, N), F32),
        scratch_shapes=[pltpu.VMEM((tk, tn), BF16)],
        compiler_params=_cp(("arbitrary", "arbitrary", "arbitrary")),
        name=name,
    )(x, w)


def _ln_kernel(*refs, alpha, n_x, ranges, want_bf16):
    x_refs = refs[:n_x]
    y_ref, g_ref, b_ref, of_ref = refs[n_x:n_x + 4]
    i = pl.program_id(0)

    def body(x_ref):
        o = _layer_norm_rows(alpha * x_ref[...] + y_ref[...], g_ref[...], b_ref[...])
        of_ref[...] = o
        if want_bf16:
            refs[n_x + 4][...] = o.astype(BF16)

    _for_row_source(i, ranges, x_refs, body)


def _deepnorm(xs, y, g, b, *, alpha, want_bf16, name):
    M, D = y.shape
    tm = _common_tile(xs, 256, 16)
    x_specs, ranges, n_m = _row_sources(xs, tm, row_axis=0)
    row = pl.BlockSpec((tm, D), lambda i: (i, 0))
    vec = pl.BlockSpec((1, D), lambda i: (0, 0))
    out_specs = [row, row] if want_bf16 else [row]
    out_shape = [jax.ShapeDtypeStruct((M, D), F32)] + ([jax.ShapeDtypeStruct((M, D), BF16)] if want_bf16 else [])
    outs = pl.pallas_call(
        functools.partial(_ln_kernel, alpha=alpha, n_x=len(xs), ranges=ranges, want_bf16=want_bf16),
        grid=(n_m,),
        in_specs=x_specs + [row, vec, vec],
        out_specs=out_specs,
        out_shape=out_shape,
        compiler_params=_cp(("arbitrary",)),
        name=name,
    )(*xs, y, g.reshape(1, D), b.reshape(1, D))
    return (outs[0], outs[1]) if want_bf16 else (outs[0], None)


def _rglru_coeffs(xc, wax_ref, ba, bx, lam, gb):
    xcb = xc.astype(BF16)
    nblk = xc.shape[1] // gb
    r_parts, i_parts = [], []
    for j in range(nblk):
        ru = _dot(xcb[:, j * gb:(j + 1) * gb], wax_ref[j])
        r_parts.append(ru[:, :gb])
        i_parts.append(ru[:, gb:])
    r = _sigmoid_tanh(jnp.concatenate(r_parts, axis=1) + ba)
    ig = _sigmoid_tanh(jnp.concatenate(i_parts, axis=1) + bx)
    log_a = (-LRU_C * r) * _softplus(-lam)
    t = jnp.tanh(0.5 * log_a)
    q = 1.0 / (1.0 - t)
    a = (1.0 + t) * q
    one_minus_a2 = (-4.0 * t) * (q * q)
    b = jnp.sqrt(one_minus_a2) * (ig * xc)
    return a, b


def _rglru_prompt_kernel(gate_ref, rnn_ref, cw_ref, cb_ref, wax_ref, ba_ref, bx_ref, lam_ref,
                         yg_ref, hlast_ref, xp_scr, h_scr, a_scr, b_scr, *, tc, gb, cw):
    b_idx = pl.program_id(1)
    i = pl.program_id(2)
    nt = pl.num_programs(2)
    halo = SUBLANES

    @pl.when(i == 0)
    def _():
        xp_scr[0:halo, :] = jnp.zeros((halo, xp_scr.shape[1]), F32)
        h_scr[...] = jnp.zeros(h_scr.shape, F32)

    xp_scr[halo:halo + tc, :] = rnn_ref[...]
    xc = cb_ref[...]
    for k in range(cw):
        xc = xc + xp_scr[pl.ds(halo - (cw - 1) + k, tc), :] * cw_ref[k:k + 1, :]
    a, b = _rglru_coeffs(xc, wax_ref, ba_ref[...], bx_ref[...], lam_ref[...], gb)
    a_scr[...] = a
    b_scr[...] = b

    def step(t, h):
        h = a_scr[pl.ds(t, 1), :] * h + b_scr[pl.ds(t, 1), :]
        b_scr[pl.ds(t, 1), :] = h
        return h

    h = lax.fori_loop(0, tc, step, h_scr[0:1, :], unroll=8)
    h_scr[0:1, :] = h
    yg_ref[...] = (b_scr[...] * gate_ref[...]).astype(BF16)
    xp_scr[0:halo, :] = xp_scr[tc:tc + halo, :]

    @pl.when(i == nt - 1)
    def _():
        hlast_ref[pl.ds(b_idx, 1), :] = h


def _rglru_prompt(u, cwt, cbias, wax, ba, bx, lam, *, batch, seq, d_rnn, gb):
    C = _pick(d_rnn, 1024, gb)
    tc = _pick(seq, 256, SUBLANES)
    nt = seq // tc
    nc = d_rnn // C
    cw = cwt.shape[0]
    vec = pl.BlockSpec((1, C), lambda c, b, i: (0, c))
    kern = functools.partial(_rglru_prompt_kernel, tc=tc, gb=gb, cw=cw)
    return pl.pallas_call(
        kern,
        grid=(nc, batch, nt),
        in_specs=[pl.BlockSpec((tc, C), lambda c, b, i: (b * nt + i, c)),
                  pl.BlockSpec((tc, C), lambda c, b, i: (b * nt + i, nc + c)),
                  pl.BlockSpec((cw, C), lambda c, b, i: (0, c)),
                  vec,
                  pl.BlockSpec((C // gb, gb, 2 * gb), lambda c, b, i: (c, 0, 0)),
                  vec, vec, vec],
        out_specs=[pl.BlockSpec((tc, C), lambda c, b, i: (b * nt + i, c)),
                   pl.BlockSpec((batch, C), lambda c, b, i: (0, c))],
        out_shape=[jax.ShapeDtypeStruct((batch * seq, d_rnn), BF16),
                   jax.ShapeDtypeStruct((batch, d_rnn), F32)],
        scratch_shapes=[pltpu.VMEM((tc + SUBLANES, C), F32), pltpu.VMEM((SUBLANES, C), F32),
                        pltpu.VMEM((tc, C), F32), pltpu.VMEM((tc, C), F32)],
        compiler_params=_cp(("arbitrary", "arbitrary", "arbitrary")),
        name="rglru_prompt",
    )(u, u, cwt, cbias, wax, ba, bx, lam)


def _rglru_sample_kernel(gate_ref, rnn_ref, cs_ref, h0_ref, cw_ref, cb_ref, wax_ref, ba_ref, bx_ref,
                         lam_ref, yg_ref, hlast_ref, xc_scr, *, steps, bsz, gb, cw):
    slabs = [cs_ref[k] for k in range(cw - 1)] + [rnn_ref[t] for t in range(steps)]
    for t in range(steps):
        xc = cb_ref[...]
        for k in range(cw):
            xc = xc + slabs[t + k] * cw_ref[k:k + 1, :]
        xc_scr[t * bsz:(t + 1) * bsz, :] = xc
    a, b = _rglru_coeffs(xc_scr[...], wax_ref, ba_ref[...], bx_ref[...], lam_ref[...], gb)
    h = h0_ref[...]
    for t in range(steps):
        h = a[t * bsz:(t + 1) * bsz, :] * h + b[t * bsz:(t + 1) * bsz, :]
        yg_ref[t] = (h * gate_ref[t]).astype(BF16)
    hlast_ref[...] = h


def _rglru_sample(gate3, rnn3, cs3, h0, cwt, cbias, wax, ba, bx, lam, *, gb):
    steps, bsz, d_rnn = rnn3.shape
    C = _pick(d_rnn, 1024, gb)
    cw = cwt.shape[0]
    vec = pl.BlockSpec((1, C), lambda c: (0, c))
    kern = functools.partial(_rglru_sample_kernel, steps=steps, bsz=bsz, gb=gb, cw=cw)
    return pl.pallas_call(
        kern,
        grid=(d_rnn // C,),
        in_specs=[pl.BlockSpec((steps, bsz, C), lambda c: (0, 0, c)),
                  pl.BlockSpec((steps, bsz, C), lambda c: (0, 0, c)),
                  pl.BlockSpec((cw - 1, bsz, C), lambda c: (0, 0, c)),
                  pl.BlockSpec((bsz, C), lambda c: (0, c)),
                  pl.BlockSpec((cw, C), lambda c: (0, c)),
                  vec,
                  pl.BlockSpec((C // gb, gb, 2 * gb), lambda c: (c, 0, 0)),
                  vec, vec, vec],
        out_specs=[pl.BlockSpec((steps, bsz, C), lambda c: (0, 0, c)),
                   pl.BlockSpec((bsz, C), lambda c: (0, c))],
        out_shape=[jax.ShapeDtypeStruct((steps, bsz, d_rnn), BF16),
                   jax.ShapeDtypeStruct((bsz, d_rnn), F32)],
        scratch_shapes=[pltpu.VMEM((steps * bsz, C), F32)],
        compiler_params=_cp(("arbitrary",)),
        name="rglru_sample",
    )(gate3, rnn3, cs3, h0, cwt, cbias, wax, ba, bx, lam)


def _pool_prompt_kernel(x_ref, pw_ref, pb_ref, ps_ref, g_ref, b_ref, of_ref, xp_scr, z_scr,
                        *, tm, batch, windows, alpha):
    bb = pl.program_id(0)
    i = pl.program_id(1)
    halo = 2 * SUBLANES
    D = x_ref.shape[1]
    pg = D // len(windows)

    @pl.when(bb < batch)
    def _():
        @pl.when(i == 0)
        def _():
            xp_scr[0:halo, :] = jnp.zeros((halo, D), F32)

        xp_scr[halo:halo + tm, :] = x_ref[...]
        pos = i * tm + lax.broadcasted_iota(jnp.int32, (tm, 1), 0)
        for gi, w in enumerate(windows):
            cols = slice(gi * pg, (gi + 1) * pg)
            xg = x_ref[:, cols]
            s = xg
            for j in range(1, w):
                s = s + xp_scr[pl.ds(halo - j, tm), cols]
            cnt = jnp.minimum(w, pos + 1).astype(F32)
            mixed = (s / cnt - xg).astype(BF16)
            out = (_dot(mixed, pw_ref[gi]) + pb_ref[:, cols]) * ps_ref[:, cols]
            z_scr[:, cols] = alpha * xg + out
        of_ref[...] = _layer_norm_rows(z_scr[...], g_ref[...], b_ref[...])
        xp_scr[0:halo, :] = xp_scr[tm:tm + halo, :]

    @pl.when(bb >= batch)
    def _():
        of_ref[...] = jnp.zeros(of_ref.shape, F32)


def _pool_prompt(x, pw, pb, ps, g, b, *, batch, seq, windows, alpha):
    M, D = x.shape
    tm = _pick(math.gcd(seq, M - batch * seq) if M > batch * seq else seq, 256, 16)
    nt = seq // tm
    n_extra = (M - batch * seq) // tm
    if n_extra > nt:
        raise ValueError("extra rows must fit in one pass of the time-tile axis")
    ng = len(windows)

    def rmap(bb, i):
        blk = jnp.where(bb < batch, bb * nt + i, batch * nt + jnp.minimum(i, max(n_extra - 1, 0)))
        return (blk, 0)

    row = pl.BlockSpec((tm, D), rmap)
    vec = pl.BlockSpec((1, D), lambda bb, i: (0, 0))
    kern = functools.partial(_pool_prompt_kernel, tm=tm, batch=batch, windows=windows, alpha=alpha)
    return pl.pallas_call(
        kern,
        grid=(batch + (1 if n_extra else 0), nt),
        in_specs=[row, pl.BlockSpec((ng, D // ng, D // ng), lambda bb, i: (0, 0, 0)), vec, vec, vec, vec],
        out_specs=row,
        out_shape=jax.ShapeDtypeStruct((M, D), F32),
        scratch_shapes=[pltpu.VMEM((tm + 2 * SUBLANES, D), F32), pltpu.VMEM((tm, D), F32)],
        compiler_params=_cp(("arbitrary", "arbitrary")),
        name="pool_prompt",
    )(x, pw, pb, ps, g, b)


def _pool_sample_kernel(st_ref, x_ref, pw_ref, pb_ref, ps_ref, g_ref, b_ref, of_ref,
                        mix_scr, z_scr, *, steps, bt, windows, pos0, alpha):
    nbuf = st_ref.shape[0]
    D = x_ref.shape[2]
    pg = D // len(windows)

    def slab(p, cols):
        return st_ref[p, :, cols] if p < nbuf else x_ref[p - nbuf, :, cols]

    for gi, w in enumerate(windows):
        cols = slice(gi * pg, (gi + 1) * pg)
        for t in range(steps):
            xg = x_ref[t, :, cols]
            s = xg
            for j in range(1, w):
                s = s + slab(nbuf + t - j, cols)
            cnt = float(min(w, pos0 + t + 1))
            mix_scr[t * bt:(t + 1) * bt, cols] = (s / cnt - xg).astype(BF16)
            z_scr[t * bt:(t + 1) * bt, cols] = alpha * xg
    for gi in range(len(windows)):
        cols = slice(gi * pg, (gi + 1) * pg)
        out = (_dot(mix_scr[:, cols], pw_ref[gi]) + pb_ref[:, cols]) * ps_ref[:, cols]
        z_scr[:, cols] = z_scr[:, cols] + out
    o = _layer_norm_rows(z_scr[...], g_ref[...], b_ref[...])
    for t in range(steps):
        of_ref[t] = o[t * bt:(t + 1) * bt, :]


def _pool_sample(st3, x3, pw, pb, ps, g, b, *, windows, pos0, alpha):
    nbuf, bsz, D = st3.shape
    steps = x3.shape[0]
    bt = _pick(bsz, 16, 16)
    ng = len(windows)
    vec = pl.BlockSpec((1, D), lambda j: (0, 0))
    blk = pl.BlockSpec((steps, bt, D), lambda j: (0, j, 0))
    kern = functools.partial(_pool_sample_kernel, steps=steps, bt=bt, windows=windows, pos0=pos0, alpha=alpha)
    return pl.pallas_call(
        kern,
        grid=(bsz // bt,),
        in_specs=[pl.BlockSpec((nbuf, bt, D), lambda j: (0, j, 0)), blk,
                  pl.BlockSpec((ng, D // ng, D // ng), lambda j: (0, 0, 0)), vec, vec, vec, vec],
        out_specs=blk,
        out_shape=jax.ShapeDtypeStruct((steps, bsz, D), F32),
        scratch_shapes=[pltpu.VMEM((steps * bt, D), BF16), pltpu.VMEM((steps * bt, D), F32)],
        compiler_params=_cp(("arbitrary",)),
        name="pool_sample",
    )(st3, x3, pw, pb, ps, g, b)


def _router_kernel(x_ref, wr_ref, sel_ref, e12_ref, w12_ref, *, n_exp):
    x = x_ref[...]
    xh = x.astype(BF16)
    xl = (x - xh.astype(F32)).astype(BF16)
    w = wr_ref[...]
    wh = w.astype(BF16)
    wl = (w - wh.astype(F32)).astype(BF16)
    logits = _dot(xh, wh) + (_dot(xh, wl) + _dot(xl, wh))
    lane = lax.broadcasted_iota(jnp.int32, logits.shape, 1)
    neg = jnp.float32(-jnp.inf)
    logits = jnp.where(lane < n_exp, logits, neg)
    m1 = jnp.max(logits, axis=-1, keepdims=True)
    i1 = jnp.min(jnp.where(logits == m1, lane, LANES), axis=-1, keepdims=True)
    rest = jnp.where(lane == i1, neg, logits)
    m2 = jnp.max(rest, axis=-1, keepdims=True)
    i2 = jnp.min(jnp.where(rest == m2, lane, LANES), axis=-1, keepdims=True)
    e = jnp.exp(m2 - m1)
    den = 1.0 + e
    w1 = 1.0 / den
    w2 = e / den
    sel_ref[...] = jnp.where((lane == i1) | (lane == i2), 1.0, 0.0).astype(F32)
    e12_ref[...] = jnp.where(lane == 0, i1, jnp.where(lane == 1, i2, 0))
    w12_ref[...] = jnp.where(lane == 0, w1, jnp.where(lane == 1, w2, 0.0))


def _router(x, wr_pad, *, n_exp):
    M, D = x.shape
    tm = _pick(M, 512, 16)
    row = pl.BlockSpec((tm, LANES), lambda i: (i, 0))
    return pl.pallas_call(
        functools.partial(_router_kernel, n_exp=n_exp),
        grid=(M // tm,),
        in_specs=[pl.BlockSpec((tm, D), lambda i: (i, 0)), pl.BlockSpec((D, LANES), lambda i: (0, 0))],
        out_specs=[row, row, row],
        out_shape=[jax.ShapeDtypeStruct((M, LANES), F32), jax.ShapeDtypeStruct((M, LANES), jnp.int32),
                   jax.ShapeDtypeStruct((M, LANES), F32)],
        compiler_params=_cp(("arbitrary",)),
        name="moe_router",
    )(x, wr_pad)


def _rank_kernel(sel_ref, rank_ref, cnt_ref, carry):
    i = pl.program_id(0)

    @pl.when(i == 0)
    def _():
        carry[...] = jnp.zeros(carry.shape, F32)

    sel = sel_ref[...]
    tm = sel.shape[0]
    r = lax.broadcasted_iota(jnp.int32, (tm, tm), 0)
    c = lax.broadcasted_iota(jnp.int32, (tm, tm), 1)
    tri = jnp.where(c < r, 1.0, 0.0).astype(BF16)
    excl = _dot(tri, sel.astype(BF16)) + carry[0:1, :]
    rank_ref[...] = excl
    tot = excl[tm - 1:tm, :] + sel[tm - 1:tm, :]
    carry[0:1, :] = tot
    cnt_ref[...] = jnp.broadcast_to(tot, cnt_ref.shape)


def _rank(sel):
    M = sel.shape[0]
    tm = _pick(M, 256, 16)
    return pl.pallas_call(
        _rank_kernel,
        grid=(M // tm,),
        in_specs=[pl.BlockSpec((tm, LANES), lambda i: (i, 0))],
        out_specs=[pl.BlockSpec((tm, LANES), lambda i: (i, 0)), pl.BlockSpec((SUBLANES, LANES), lambda i: (0, 0))],
        out_shape=[jax.ShapeDtypeStruct((M, LANES), F32), jax.ShapeDtypeStruct((SUBLANES, LANES), F32)],
        scratch_shapes=[pltpu.VMEM((SUBLANES, LANES), F32)],
        compiler_params=_cp(("arbitrary",)),
        name="moe_rank",
    )(sel)


def _slot_kernel(rank_ref, e12_ref, base_ref, pos_ref):
    lane = lax.broadcasted_iota(jnp.int32, rank_ref.shape, 1)
    slot = rank_ref[...] + base_ref[...]
    e12 = e12_ref[...]
    p1 = jnp.sum(jnp.where(lane == e12[:, 0:1], slot, 0.0), axis=-1, keepdims=True)
    p2 = jnp.sum(jnp.where(lane == e12[:, 1:2], slot, 0.0), axis=-1, keepdims=True)
    pos_ref[...] = jnp.where(lane == 0, p1, jnp.where(lane == 1, p2, 0.0)).astype(jnp.int32)


def _slots(rank, e12, base_row):
    M = rank.shape[0]
    tm = _pick(M, 512, 16)
    row = pl.BlockSpec((tm, LANES), lambda i: (i, 0))
    return pl.pallas_call(
        _slot_kernel,
        grid=(M // tm,),
        in_specs=[row, row, pl.BlockSpec((1, LANES), lambda i: (0, 0))],
        out_specs=row,
        out_shape=jax.ShapeDtypeStruct((M, LANES), jnp.int32),
        compiler_params=_cp(("arbitrary",)),
        name="moe_slots",
    )(rank, e12, base_row)


def _invert_kernel(p1_ref, p2_ref, src_ref, *, n_tok, n_slot):
    def zero(s, c):
        src_ref[s] = 0
        return c

    lax.fori_loop(0, n_slot, zero, 0, unroll=8)

    def put(t, c):
        src_ref[p1_ref[t]] = t
        src_ref[p2_ref[t]] = t
        return c

    lax.fori_loop(0, n_tok, put, 0, unroll=8)


def _invert(p1, p2, n_slot):
    n_tok = p1.shape[0]
    smem = pl.BlockSpec(memory_space=pltpu.SMEM)
    return pl.pallas_call(
        functools.partial(_invert_kernel, n_tok=n_tok, n_slot=n_slot),
        in_specs=[smem, smem],
        out_specs=smem,
        out_shape=jax.ShapeDtypeStruct((n_slot,), jnp.int32),
        name="moe_invert",
    )(p1, p2)


def _row_copy(src_hbm, row, dst, i, sem):
    return pltpu.make_async_copy(src_hbm.at[pl.ds(row, 1), :], dst.at[pl.ds(i, 1), :], sem)


def _tile_wait(src_hbm, dst, sem):
    pltpu.make_async_copy(src_hbm.at[pl.ds(0, dst.shape[0]), :], dst, sem).wait()


def _dispatch_kernel(valid_ref, src_ref, nsrc_ref, x_hbm, out_ref, buf, sem, *, rows):
    q = pl.program_id(0)
    nq = pl.num_programs(0)

    def start(idx_ref, slot):
        def issue(i, c):
            _row_copy(x_hbm, idx_ref[0, 0, i], buf.at[slot], i, sem.at[slot]).start()
            return c

        lax.fori_loop(0, rows, issue, 0, unroll=8)

    @pl.when((q == 0) & (valid_ref[0] > 0))
    def _():
        start(src_ref, 0)

    nxt = jnp.minimum(q + 1, nq - 1)

    @pl.when((q + 1 < nq) & (valid_ref[nxt] > 0))
    def _():
        start(nsrc_ref, (q + 1) % 2)

    slot = q % 2

    @pl.when(valid_ref[q] > 0)
    def _():
        _tile_wait(x_hbm, buf.at[slot], sem.at[slot])
        out_ref[...] = buf[slot].astype(BF16)

    @pl.when(valid_ref[q] == 0)
    def _():
        out_ref[...] = jnp.zeros(out_ref.shape, BF16)


def _dispatch(x, src, tile_valid):
    n_slot = src.shape[0]
    D = x.shape[1]
    rows = SUB_ROWS
    nt = n_slot // rows
    src3 = src.reshape(nt, 1, rows)
    return pl.pallas_call(
        functools.partial(_dispatch_kernel, rows=rows),
        grid_spec=pltpu.PrefetchScalarGridSpec(
            num_scalar_prefetch=1,
            grid=(nt,),
            in_specs=[pl.BlockSpec((1, 1, rows), lambda q, v: (q, 0, 0), memory_space=pltpu.SMEM),
                      pl.BlockSpec((1, 1, rows), lambda q, v: (jnp.minimum(q + 1, nt - 1), 0, 0),
                                   memory_space=pltpu.SMEM),
                      pl.BlockSpec(memory_space=pl.ANY)],
            out_specs=pl.BlockSpec((rows, D), lambda q, v: (q, 0)),
            scratch_shapes=[pltpu.VMEM((2, rows, D), F32), pltpu.SemaphoreType.DMA((2,))],
        ),
        out_shape=jax.ShapeDtypeStruct((n_slot, D), BF16),
        compiler_params=_cp(("arbitrary",)),
        name="moe_dispatch",
    )(tile_valid, src3, src3, x)


def _combine_kernel(p1_ref, p2_ref, np1_ref, np2_ref, x_ref, w12_ref, g_ref, b_ref, y_hbm, *refs,
                    rows, alpha, bounds):
    n_out = len(bounds)
    out_refs = refs[:n_out]
    ybuf, sem = refs[n_out:]
    i = pl.program_id(0)
    n = pl.num_programs(0)

    def start(a_ref, b_ref_, slot):
        def issue(r, c):
            _row_copy(y_hbm, a_ref[0, 0, r], ybuf.at[slot, 0], r, sem.at[slot]).start()
            _row_copy(y_hbm, b_ref_[0, 0, r], ybuf.at[slot, 1], r, sem.at[slot]).start()
            return c

        lax.fori_loop(0, rows, issue, 0, unroll=8)

    @pl.when(i == 0)
    def _():
        start(p1_ref, p2_ref, 0)

    @pl.when(i + 1 < n)
    def _():
        start(np1_ref, np2_ref, (i + 1) % 2)

    slot = i % 2
    _tile_wait(y_hbm, ybuf.at[slot, 0], sem.at[slot])
    _tile_wait(y_hbm, ybuf.at[slot, 1], sem.at[slot])
    w12 = w12_ref[...]
    ffn = w12[:, 0:1] * ybuf[slot, 0] + w12[:, 1:2] * ybuf[slot, 1]
    res = _layer_norm_rows(alpha * x_ref[...] + ffn, g_ref[...], b_ref[...])
    if n_out == 1:
        out_refs[0][...] = res
    else:
        for (lo, hi), o_ref in zip(bounds, out_refs):
            @pl.when((i >= lo) & (i < hi))
            def _(o_ref=o_ref):
                o_ref[...] = res


def _combine(x, y_sorted, pos1, pos2, w12, g, b, *, alpha, splits):
    M, D = x.shape
    rows = _pick(math.gcd(*splits), SUB_ROWS, 16)
    nt = M // rows
    p1 = pos1.reshape(nt, 1, rows)
    p2 = pos2.reshape(nt, 1, rows)
    idx = pl.BlockSpec((1, 1, rows), lambda i: (i, 0, 0), memory_space=pltpu.SMEM)
    nidx = pl.BlockSpec((1, 1, rows), lambda i: (jnp.minimum(i + 1, nt - 1), 0, 0), memory_space=pltpu.SMEM)
    row = pl.BlockSpec((rows, D), lambda i: (i, 0))
    vec = pl.BlockSpec((1, D), lambda i: (0, 0))
    bounds, out_specs, out_shape = [], [], []
    lo = 0
    for s in splits:
        nb = s // rows
        bounds.append((lo, lo + nb))
        out_specs.append(pl.BlockSpec((rows, D), lambda i, lo=lo, nb=nb: (jnp.clip(i - lo, 0, nb - 1), 0)))
        out_shape.append(jax.ShapeDtypeStruct((s, D), F32))
        lo += nb
    return pl.pallas_call(
        functools.partial(_combine_kernel, rows=rows, alpha=alpha, bounds=tuple(bounds)),
        grid=(nt,),
        in_specs=[idx, idx, nidx, nidx, row, pl.BlockSpec((rows, LANES), lambda i: (i, 0)), vec, vec,
                  pl.BlockSpec(memory_space=pl.ANY)],
        out_specs=out_specs,
        out_shape=out_shape,
        scratch_shapes=[pltpu.VMEM((2, 2, rows, D), F32), pltpu.SemaphoreType.DMA((2,))],
        compiler_params=_cp(("arbitrary",)),
        name="moe_combine",
    )(p1, p2, p1, p2, x, w12, g.reshape(1, D), b.reshape(1, D), y_sorted)


def _piece_loop(nsub, big, fn):
    nbig = lax.div(nsub, jnp.int32(big))

    def big_body(p, c):
        fn(pl.multiple_of(p * (big * SUB_ROWS), big * SUB_ROWS), big * SUB_ROWS)
        return c

    lax.fori_loop(0, nbig, big_body, 0)

    def small_body(s, c):
        fn(pl.multiple_of(s * SUB_ROWS, SUB_ROWS), SUB_ROWS)
        return c

    lax.fori_loop(nbig * big, nsub, small_body, 0)


def _zero_tail(ref, nsub, n_sub):
    def fill(s, c):
        r0 = pl.multiple_of(s * SUB_ROWS, SUB_ROWS)
        ref[pl.ds(r0, SUB_ROWS), :] = jnp.zeros((SUB_ROWS, ref.shape[1]), ref.dtype)
        return c

    lax.fori_loop(nsub, n_sub, fill, 0)


def _moe_up_kernel(exp_ref, nsub_ref, csrc_ref, x_ref, wg_ref, wu_ref, h_ref, wgb, wub, *, n_sub, big):
    c = pl.program_id(0)
    nsub = nsub_ref[c]

    @pl.when(nsub > 0)
    def _():
        wgb[...] = wg_ref[...].astype(BF16)
        wub[...] = wu_ref[...].astype(BF16)

        def piece(r0, rows):
            x = x_ref[pl.ds(r0, rows), :]
            g = _dot(x, wgb[...])
            u = _dot(x, wub[...])
            h_ref[pl.ds(r0, rows), :] = (g * jax.nn.sigmoid(g) * u).astype(BF16)

        _piece_loop(nsub, big, piece)
        _zero_tail(h_ref, nsub, n_sub)


def _moe_up(xs, wg, wu, ch_exp, ch_nsub, ch_src, *, tm_chunk, tf):
    n_slot, D = xs.shape
    F = wg.shape[2]
    tf = _pick(F, tf, LANES)
    nf = F // tf
    nch = n_slot // tm_chunk
    n_sub = tm_chunk // SUB_ROWS

    def fidx(c, f, nsub):
        return jnp.where(nsub[c] > 0, f, nf - 1)

    return pl.pallas_call(
        functools.partial(_moe_up_kernel, n_sub=n_sub, big=min(BIG_SUBS, n_sub)),
        grid_spec=pltpu.PrefetchScalarGridSpec(
            num_scalar_prefetch=3,
            grid=(nch, nf),
            in_specs=[pl.BlockSpec((tm_chunk, D), lambda c, f, e, ns, cs: (cs[c], 0),
                                   pipeline_mode=pl.Buffered(1)),
                      pl.BlockSpec((None, D, tf), lambda c, f, e, ns, cs: (e[c], 0, fidx(c, f, ns))),
                      pl.BlockSpec((None, D, tf), lambda c, f, e, ns, cs: (e[c], 0, fidx(c, f, ns)))],
            out_specs=pl.BlockSpec((tm_chunk, tf), lambda c, f, e, ns, cs: (cs[c], fidx(c, f, ns))),
            scratch_shapes=[pltpu.VMEM((D, tf), BF16), pltpu.VMEM((D, tf), BF16)],
        ),
        out_shape=jax.ShapeDtypeStruct((n_slot, F), BF16),
        compiler_params=_cp(("arbitrary", "arbitrary")),
        name="moe_up",
    )(ch_exp, ch_nsub, ch_src, xs, wg, wu)


def _moe_down_kernel(exp_ref, nsub_ref, csrc_ref, h_ref, wd_ref, y_ref, wdb, *, n_sub, big):
    c = pl.program_id(0)
    k = pl.program_id(2)
    nsub = nsub_ref[c]

    @pl.when(nsub > 0)
    def _():
        wdb[...] = wd_ref[...].astype(BF16)

        @pl.when(k == 0)
        def _():
            def first(r0, rows):
                y_ref[pl.ds(r0, rows), :] = _dot(h_ref[pl.ds(r0, rows), :], wdb[...])

            _piece_loop(nsub, big, first)
            _zero_tail(y_ref, nsub, n_sub)

        @pl.when(k > 0)
        def _():
            def acc(r0, rows):
                y_ref[pl.ds(r0, rows), :] += _dot(h_ref[pl.ds(r0, rows), :], wdb[...])

            _piece_loop(nsub, big, acc)


def _moe_down(hs, wd, ch_exp, ch_nsub, ch_src, *, tm_chunk, tn, tk):
    n_slot, F = hs.shape
    D = wd.shape[2]
    tn = _pick(D, tn, LANES)
    tk = _pick(F, tk, LANES)
    nn, nk = D // tn, F // tk
    nch = n_slot // tm_chunk
    n_sub = tm_chunk // SUB_ROWS

    def nidx(c, n, ns):
        return jnp.where(ns[c] > 0, n, nn - 1)

    def kidx(c, k, ns):
        return jnp.where(ns[c] > 0, k, nk - 1)

    return pl.pallas_call(
        functools.partial(_moe_down_kernel, n_sub=n_sub, big=min(BIG_SUBS, n_sub)),
        grid_spec=pltpu.PrefetchScalarGridSpec(
            num_scalar_prefetch=3,
            grid=(nch, nn, nk),
            in_specs=[pl.BlockSpec((tm_chunk, tk), lambda c, n, k, e, ns, cs: (cs[c], kidx(c, k, ns))),
                      pl.BlockSpec((None, tk, tn),
                                   lambda c, n, k, e, ns, cs: (e[c], kidx(c, k, ns), nidx(c, n, ns)))],
            out_specs=pl.BlockSpec((tm_chunk, tn), lambda c, n, k, e, ns, cs: (cs[c], nidx(c, n, ns))),
            scratch_shapes=[pltpu.VMEM((tk, tn), BF16)],
        ),
        out_shape=jax.ShapeDtypeStruct((n_slot, D), F32),
        compiler_params=_cp(("arbitrary", "arbitrary", "arbitrary")),
        name="moe_down",
    )(ch_exp, ch_nsub, ch_src, hs, wd)


def _chunk_tables(cnt, *, n_exp, tm_chunk, n_chunk):
    cnt = cnt.astype(jnp.int32)
    nch_e = (cnt + tm_chunk - 1) // tm_chunk
    cend = jnp.cumsum(nch_e)
    cstart = cend - nch_e
    total = cend[-1]
    j = jnp.arange(n_chunk, dtype=jnp.int32)
    jj = jnp.minimum(j, total - 1)
    e = jnp.minimum(jnp.sum((jj[:, None] >= cend[None, :]).astype(jnp.int32), axis=1), n_exp - 1)
    local = jj - cstart[e]
    rows = jnp.clip(cnt[e] - local * tm_chunk, 0, tm_chunk)
    nsub = jnp.where(j < total, (rows + SUB_ROWS - 1) // SUB_ROWS, 0).astype(jnp.int32)
    base = (cstart * tm_chunk).astype(F32)
    return e.astype(jnp.int32), nsub, jj.astype(jnp.int32), base


def _moe_ffn(x_f32, w_router, wg, wu, wd, ln_g, ln_b, *, alpha, splits):
    M, D = x_f32.shape
    n_exp = wg.shape[0]
    avg = TOP_K * M / n_exp
    tm_chunk = SUB_ROWS * max(1, -(-int(avg * 1.11) // SUB_ROWS))
    n_chunk = n_exp + (TOP_K * M) // tm_chunk
    n_slot = n_chunk * tm_chunk

    wr_pad = jnp.pad(w_router, ((0, 0), (0, LANES - n_exp)))
    sel, e12, w12 = _router(x_f32, wr_pad, n_exp=n_exp)
    rank, cnt = _rank(sel)
    ch_exp, ch_nsub, ch_src, base = _chunk_tables(cnt[0, :n_exp], n_exp=n_exp, tm_chunk=tm_chunk,
                                                  n_chunk=n_chunk)
    base_row = jnp.pad(base, (0, LANES - n_exp)).reshape(1, LANES)
    pos = _slots(rank, e12, base_row)
    pos1, pos2 = pos[:, 0], pos[:, 1]
    src = _invert(pos1, pos2, n_slot)
    sub_per_chunk = tm_chunk // SUB_ROWS
    tile_valid = (jnp.arange(sub_per_chunk, dtype=jnp.int32)[None, :] < ch_nsub[:, None])
    tile_valid = tile_valid.astype(jnp.int32).reshape(-1)
    xs = _dispatch(x_f32, src, tile_valid)
    hs = _moe_up(xs, wg, wu, ch_exp, ch_nsub, ch_src, tm_chunk=tm_chunk, tf=256)
    ys = _moe_down(hs, wd, ch_exp, ch_nsub, ch_src, tm_chunk=tm_chunk, tn=1024, tk=1024)
    return _combine(x_f32, ys, pos1, pos2, w12, ln_g, ln_b, alpha=alpha, splits=splits)


def _last_rows(x2d, batch, seq, n, col0, col1):
    return jnp.stack([lax.slice(x2d, (b * seq + seq - n, col0), ((b + 1) * seq, col1)) for b in range(batch)])


def kernel(x_prompt, x_sample, state_rglru_h, state_rglru_conv, state_pool, rg_w_in, rg_conv_w, rg_conv_b,
           rg_w_a, rg_b_a, rg_w_x, rg_b_x, rg_lambda, rg_w_out, pool_w, pool_b, pool_scale, ffn_w_gate,
           ffn_w_up, ffn_w_down, moe_router, moe_w_gate, moe_w_up, moe_w_down, ln_mix_g, ln_mix_b,
           ln_ffn_g, ln_ffn_b):
    B, S, D = x_prompt.shape
    Bs, Ss, _ = x_sample.shape
    depth = ln_mix_g.shape[0]
    alpha = float((2 * depth) ** 0.25)
    d_rnn = rg_w_a.shape[1] * rg_w_a.shape[2]
    gb = rg_w_a.shape[2]
    n_pool_groups = pool_w.shape[1]
    windows = tuple(2 ** (i + 1) for i in range(n_pool_groups))
    pool_buf = state_pool.shape[2]
    conv_w = rg_conv_w.shape[1]
    Mp, Ms = B * S, Bs * Ss

    xp2 = x_prompt.reshape(Mp, D)
    xs2 = jnp.swapaxes(x_sample, 0, 1).reshape(Ms, D)
    x_parts = [xp2, xs2]
    xb = jnp.concatenate([xp2.astype(BF16), xs2.astype(BF16)], axis=0)

    new_h_p, new_conv_p, new_pool_p = [], [], []
    new_h_s, new_conv_s, new_pool_s = [], [], []
    y_parts = None
    for i in range(depth):
        j = i // 2
        last = i == depth - 1
        if i % 2 == 0:
            u = _mm_fullk([xb], [rg_w_in[j]], kind="gelu_split", out_dtype=F32, tm=1024, tn=512,
                          n_gelu_cols=d_rnn, name="rg_in_proj")
            wax = jnp.concatenate([rg_w_a[j], rg_w_x[j]], axis=-1).astype(BF16)
            vec = lambda v: v.reshape(1, d_rnn)
            args = (rg_conv_w[j], vec(rg_conv_b[j]), wax, vec(rg_b_a[j]), vec(rg_b_x[j]), vec(rg_lambda[j]))
            yg_p, h_p = _rglru_prompt(u, *args, batch=B, seq=S, d_rnn=d_rnn, gb=gb)
            u_s = lax.slice(u, (Mp, 0), (Mp + Ms, 2 * d_rnn)).reshape(Ss, Bs, 2 * d_rnn)
            cs3 = jnp.swapaxes(state_rglru_conv[j], 0, 1)
            rnn_s = u_s[:, :, d_rnn:]
            yg_s, h_s = _rglru_sample(u_s[:, :, :d_rnn], rnn_s, cs3, state_rglru_h[j], *args, gb=gb)
            new_h_p.append(h_p)
            new_h_s.append(h_s)
            new_conv_p.append(_last_rows(u, B, S, conv_w - 1, d_rnn, 2 * d_rnn))
            new_conv_s.append(jnp.swapaxes(jnp.concatenate([cs3, rnn_s], axis=0)[-(conv_w - 1):], 0, 1))
            mix = _mm_fullk([yg_p, yg_s.reshape(Ms, d_rnn)], [rg_w_out[j]], kind="plain", out_dtype=F32,
                            tm=1024, tn=512, name="rg_out_proj")
            x, xb = _deepnorm(x_parts, mix, ln_mix_g[i], ln_mix_b[i], alpha=alpha, want_bf16=True, name="ln_mix")
            hid = _mm_fullk([xb], [ffn_w_gate[j], ffn_w_up[j]], kind="swiglu", out_dtype=BF16, tm=1024, tn=256,
                            name="ffn_up")
            ffn = _mm_ktiled(hid, ffn_w_down[j], tm=2304, tn=1024, tk=1024, name="ffn_down")
            x, xb = _deepnorm([x], ffn, ln_ffn_g[i], ln_ffn_b[i], alpha=alpha, want_bf16=not last, name="ln_ffn")
            x_parts = [x]
        else:
            if len(x_parts) != 1:
                x = jnp.concatenate(x_parts, axis=0)
            pw = pool_w[j].astype(BF16)
            vec = lambda v: v.reshape(1, D)
            pargs = (pw, vec(pool_b[j]), vec(pool_scale[j]), vec(ln_mix_g[i]), vec(ln_mix_b[i]))
            x_s3 = lax.slice(x, (Mp, 0), (Mp + Ms, D)).reshape(Ss, Bs, D)
            st3 = jnp.swapaxes(state_pool[j], 0, 1)
            xm = _pool_prompt(x, *pargs, batch=B, seq=S, windows=windows, alpha=alpha)
            xs_f = _pool_sample(st3, x_s3, *pargs, windows=windows, pos0=PAST_LEN, alpha=alpha)
            new_pool_p.append(_last_rows(x, B, S, pool_buf, 0, D))
            new_pool_s.append(jnp.swapaxes(jnp.concatenate([st3, x_s3], axis=0)[-pool_buf:], 0, 1))
            xm = lax.dynamic_update_slice(xm, xs_f.reshape(Ms, D), (Mp, 0))
            outs = _moe_ffn(xm, moe_router[j], moe_w_gate[j], moe_w_up[j], moe_w_down[j],
                            ln_ffn_g[i], ln_ffn_b[i], alpha=alpha, splits=(Mp, Ms))
            x_parts = list(outs)
            if not last:
                xb = jnp.concatenate([o.astype(BF16) for o in outs], axis=0)

    if len(x_parts) == 1:
        x_parts = [x_parts[0][:Mp], x_parts[0][Mp:]]
    y_prompt = x_parts[0].reshape(B, S, D)
    y_sample = jnp.swapaxes(x_parts[1].reshape(Ss, Bs, D), 0, 1)
    return (y_prompt, y_sample, jnp.stack(new_h_p), jnp.stack(new_conv_p), jnp.stack(new_pool_p),
            jnp.stack(new_h_s), jnp.stack(new_conv_s), jnp.stack(new_pool_s))
```

```python
import functools
import math

import jax
import jax.numpy as jnp
from jax import lax
from jax.experimental import pallas as pl
from jax.experimental.pallas import tpu as pltpu

F32 = jnp.float32
BF16 = jnp.bfloat16

LRU_C = 8.0
LN_EPS = 1e-5
PAST_LEN = 16384
TOP_K = 2
GELU_C = 0.7978845608028654

LANES = 128
SUBLANES = 8
SUB_ROWS = 256
BIG_SUBS = 4
VMEM_LIMIT = 56 * 1024 * 1024


def _pick(n, target, mult):
    best = None
    d = mult
    while d <= min(n, target):
        if n % d == 0:
            best = d
        d += mult
    return best if best is not None else n


def _cp(sem):
    return pltpu.CompilerParams(dimension_semantics=sem, vmem_limit_bytes=VMEM_LIMIT)


def _dot(a, b):
    return jnp.dot(a, b, preferred_element_type=F32)


def _gelu_tanh(x):
    return 0.5 * x * (1.0 + jnp.tanh(GELU_C * (x + 0.044715 * (x * x * x))))


def _softplus(z):
    return jnp.log1p(jnp.exp(-jnp.abs(z))) + jnp.maximum(z, 0.0)


def _sigmoid_tanh(x):
    return 0.5 + 0.5 * jnp.tanh(0.5 * x)


def _layer_norm_rows(z, g, b):
    mu = jnp.mean(z, axis=-1, keepdims=True)
    zc = z - mu
    var = jnp.mean(zc * zc, axis=-1, keepdims=True)
    return zc * lax.rsqrt(var + LN_EPS) * g + b


def _row_sources(arrs, tm, row_axis):
    specs, ranges = [], []
    lo = 0
    for a in arrs:
        nb = a.shape[0] // tm
        if nb * tm != a.shape[0]:
            raise ValueError("row source not divisible by the row tile")

        def imap(*idx, lo=lo, nb=nb):
            return (jnp.clip(idx[row_axis] - lo, 0, nb - 1), 0)

        specs.append(pl.BlockSpec((tm, a.shape[1]), imap))
        ranges.append((lo, lo + nb))
        lo += nb
    return specs, ranges, lo


def _for_row_source(m, ranges, refs, body):
    if len(refs) == 1:
        body(refs[0])
        return
    for (lo, hi), r in zip(ranges, refs):
        @pl.when((m >= lo) & (m < hi))
        def _(r=r):
            body(r)


def _common_tile(arrs, target, mult):
    return _pick(math.gcd(*[a.shape[0] for a in arrs]), target, mult)


def _cast_kernel(*refs, n_x, ranges):
    o_ref = refs[n_x]

    def body(x_ref):
        o_ref[...] = x_ref[...].astype(BF16)

    _for_row_source(pl.program_id(0), ranges, refs[:n_x], body)


def _cast_rows(xs):
    D = xs[0].shape[1]
    tm = _common_tile(xs, 512, 16)
    x_specs, ranges, n_m = _row_sources(xs, tm, row_axis=0)
    return pl.pallas_call(
        functools.partial(_cast_kernel, n_x=len(xs), ranges=ranges),
        grid=(n_m,),
        in_specs=x_specs,
        out_specs=pl.BlockSpec((tm, D), lambda i: (i, 0)),
        out_shape=jax.ShapeDtypeStruct((n_m * tm, D), BF16),
        compiler_params=_cp(("arbitrary",)),
        name="cast_rows",
    )(*xs)


def _mm_fullk_kernel(*refs, kind, n_x, n_w, ranges, n_gelu_tiles):
    x_refs = refs[:n_x]
    w_refs = refs[n_x:n_x + n_w]
    out_ref = refs[n_x + n_w]
    wbf = refs[n_x + n_w + 1:]
    n = pl.program_id(0)
    m = pl.program_id(1)

    @pl.when(m == 0)
    def _():
        for w_ref, s in zip(w_refs, wbf):
            s[...] = w_ref[...].astype(BF16)

    def body(x_ref):
        x = x_ref[...]
        if kind == "swiglu":
            g = _dot(x, wbf[0][...])
            u = _dot(x, wbf[1][...])
            out_ref[...] = (g * jax.nn.sigmoid(g) * u).astype(out_ref.dtype)
        elif kind == "gelu_split":
            u = _dot(x, wbf[0][...])

            @pl.when(n < n_gelu_tiles)
            def _():
                out_ref[...] = _gelu_tanh(u)

            @pl.when(n >= n_gelu_tiles)
            def _():
                out_ref[...] = u
        else:
            out_ref[...] = _dot(x, wbf[0][...]).astype(out_ref.dtype)

    _for_row_source(m, ranges, x_refs, body)


def _mm_fullk(xs, ws, *, kind, out_dtype, tm, tn, n_gelu_cols=0, name):
    K = xs[0].shape[1]
    N = ws[0].shape[1]
    tm = _common_tile(xs, tm, 16)
    tn = _pick(N, tn, LANES)
    x_specs, ranges, n_m = _row_sources(xs, tm, row_axis=1)
    kern = functools.partial(_mm_fullk_kernel, kind=kind, n_x=len(xs), n_w=len(ws), ranges=ranges,
                             n_gelu_tiles=n_gelu_cols // tn)
    return pl.pallas_call(
        kern,
        grid=(N // tn, n_m),
        in_specs=x_specs + [pl.BlockSpec((K, tn), lambda n, m: (0, n)) for _ in ws],
        out_specs=pl.BlockSpec((tm, tn), lambda n, m: (m, n)),
        out_shape=jax.ShapeDtypeStruct((n_m * tm, N), out_dtype),
        scratch_shapes=[pltpu.VMEM((K, tn), BF16) for _ in ws],
        compiler_params=_cp(("arbitrary", "arbitrary")),
        name=name,
    )(*xs, *ws)


def _mm_ktiled_kernel(x_ref, w_ref, o_ref, wbf, *, n_piece):
    k = pl.program_id(2)
    wbf[...] = w_ref[...].astype(BF16)
    rows = x_ref.shape[0] // n_piece

    @pl.when(k == 0)
    def _():
        for p in range(n_piece):
            o_ref[p * rows:(p + 1) * rows, :] = _dot(x_ref[p * rows:(p + 1) * rows, :], wbf[...])

    @pl.when(k > 0)
    def _():
        for p in range(n_piece):
            o_ref[p * rows:(p + 1) * rows, :] += _dot(x_ref[p * rows:(p + 1) * rows, :], wbf[...])


def _mm_ktiled(x, w, *, tm, tn, tk, name):
    M, K = x.shape
    N = w.shape[1]
    tm = _pick(M, tm, 16)
    tn = _pick(N, tn, LANES)
    tk = _pick(K, tk, LANES)
    n_piece = 3 if tm % (3 * 16) == 0 else 1
    return pl.pallas_call(
        functools.partial(_mm_ktiled_kernel, n_piece=n_piece),
        grid=(M // tm, N // tn, K // tk),
        in_specs=[pl.BlockSpec((tm, tk), lambda m, n, k: (m, k)),
                  pl.BlockSpec((tk, tn), lambda m, n, k: (k, n))],
        out_specs=pl.BlockSpec((tm, tn), lambda m, n, k: (m, n)),
        out_shape=jax.ShapeDtypeStruct((M, N), F32),
        scratch_shapes=[pltpu.VMEM((tk, tn), BF16)],
        compiler_params=_cp(("arbitrary", "arbitrary", "arbitrary")),
        name=name,
    )(x, w)


def _ln_kernel(*refs, alpha, n_x, ranges, want_bf16):
    x_refs = refs[:n_x]
    y_ref, g_ref, b_ref, of_ref = refs[n_x:n_x + 4]
    i = pl.program_id(0)

    def body(x_ref):
        o = _layer_norm_rows(alpha * x_ref[...] + y_ref[...], g_ref[...], b_ref[...])
        of_ref[...] = o
        if want_bf16:
            refs[n_x + 4][...] = o.astype(BF16)

    _for_row_source(i, ranges, x_refs, body)


def _deepnorm(xs, y, g, b, *, alpha, want_bf16, name):
    M, D = y.shape
    tm = _common_tile(xs, 256, 16)
    x_specs, ranges, n_m = _row_sources(xs, tm, row_axis=0)
    row = pl.BlockSpec((tm, D), lambda i: (i, 0))
    vec = pl.BlockSpec((1, D), lambda i: (0, 0))
    out_specs = [row, row] if want_bf16 else [row]
    out_shape = [jax.ShapeDtypeStruct((M, D), F32)] + ([jax.ShapeDtypeStruct((M, D), BF16)] if want_bf16 else [])
    outs = pl.pallas_call(
        functools.partial(_ln_kernel, alpha=alpha, n_x=len(xs), ranges=ranges, want_bf16=want_bf16),
        grid=(n_m,),
        in_specs=x_specs + [row, vec, vec],
        out_specs=out_specs,
        out_shape=out_shape,
        compiler_params=_cp(("arbitrary",)),
        name=name,
    )(*xs, y, g.reshape(1, D), b.reshape(1, D))
    return (outs[0], outs[1]) if want_bf16 else (outs[0], None)


def _rglru_coeffs(xc, wax_ref, ba, bx, lam, gb):
    xcb = xc.astype(BF16)
    nblk = xc.shape[1] // gb
    r_parts, i_parts = [], []
    for j in range(nblk):
        ru = _dot(xcb[:, j * gb:(j + 1) * gb], wax_ref[j])
        r_parts.append(ru[:, :gb])
        i_parts.append(ru[:, gb:])
    r = _sigmoid_tanh(jnp.concatenate(r_parts, axis=1) + ba)
    ig = _sigmoid_tanh(jnp.concatenate(i_parts, axis=1) + bx)
    log_a = (-LRU_C * r) * _softplus(-lam)
    t = jnp.tanh(0.5 * log_a)
    q = 1.0 / (1.0 - t)
    a = (1.0 + t) * q
    one_minus_a2 = (-4.0 * t) * (q * q)
    b = jnp.sqrt(one_minus_a2) * (ig * xc)
    return a, b


def _rglru_prompt_kernel(gate_ref, rnn_ref, cw_ref, cb_ref, wax_ref, ba_ref, bx_ref, lam_ref,
                         yg_ref, hlast_ref, xp_scr, h_scr, a_scr, b_scr, *, tc, gb, cw):
    b_idx = pl.program_id(1)
    i = pl.program_id(2)
    nt = pl.num_programs(2)
    halo = SUBLANES

    @pl.when(i == 0)
    def _():
        xp_scr[0:halo, :] = jnp.zeros((halo, xp_scr.shape[1]), F32)
        h_scr[...] = jnp.zeros(h_scr.shape, F32)

    xp_scr[halo:halo + tc, :] = rnn_ref[...]
    xc = cb_ref[...]
    for k in range(cw):
        xc = xc + xp_scr[pl.ds(halo - (cw - 1) + k, tc), :] * cw_ref[k:k + 1, :]
    a, b = _rglru_coeffs(xc, wax_ref, ba_ref[...], bx_ref[...], lam_ref[...], gb)
    a_scr[...] = a
    b_scr[...] = b

    def step(t, h):
        h = a_scr[pl.ds(t, 1), :] * h + b_scr[pl.ds(t, 1), :]
        b_scr[pl.ds(t, 1), :] = h
        return h

    h = lax.fori_loop(0, tc, step, h_scr[0:1, :], unroll=8)
    h_scr[0:1, :] = h
    yg_ref[...] = (b_scr[...] * gate_ref[...]).astype(BF16)
    xp_scr[0:halo, :] = xp_scr[tc:tc + halo, :]

    @pl.when(i == nt - 1)
    def _():
        hlast_ref[pl.ds(b_idx, 1), :] = h


def _rglru_prompt(u, cwt, cbias, wax, ba, bx, lam, *, batch, seq, d_rnn, gb):
    C = _pick(d_rnn, 1024, gb)
    tc = _pick(seq, 256, SUBLANES)
    nt = seq // tc
    nc = d_rnn // C
    cw = cwt.shape[0]
    vec = pl.BlockSpec((1, C), lambda c, b, i: (0, c))
    kern = functools.partial(_rglru_prompt_kernel, tc=tc, gb=gb, cw=cw)
    return pl.pallas_call(
        kern,
        grid=(nc, batch, nt),
        in_specs=[pl.BlockSpec((tc, C), lambda c, b, i: (b * nt + i, c)),
                  pl.BlockSpec((tc, C), lambda c, b, i: (b * nt + i, nc + c)),
                  pl.BlockSpec((cw, C), lambda c, b, i: (0, c)),
                  vec,
                  pl.BlockSpec((C // gb, gb, 2 * gb), lambda c, b, i: (c, 0, 0)),
                  vec, vec, vec],
        out_specs=[pl.BlockSpec((tc, C), lambda c, b, i: (b * nt + i, c)),
                   pl.BlockSpec((batch, C), lambda c, b, i: (0, c))],
        out_shape=[jax.ShapeDtypeStruct((batch * seq, d_rnn), BF16),
                   jax.ShapeDtypeStruct((batch, d_rnn), F32)],
        scratch_shapes=[pltpu.VMEM((tc + SUBLANES, C), F32), pltpu.VMEM((SUBLANES, C), F32),
                        pltpu.VMEM((tc, C), F32), pltpu.VMEM((tc, C), F32)],
        compiler_params=_cp(("arbitrary", "arbitrary", "arbitrary")),
        name="rglru_prompt",
    )(u, u, cwt, cbias, wax, ba, bx, lam)


def _rglru_sample_kernel(gate_ref, rnn_ref, cs_ref, h0_ref, cw_ref, cb_ref, wax_ref, ba_ref, bx_ref,
                         lam_ref, yg_ref, hlast_ref, xc_scr, *, steps, bsz, gb, cw):
    slabs = [cs_ref[k] for k in range(cw - 1)] + [rnn_ref[t] for t in range(steps)]
    for t in range(steps):
        xc = cb_ref[...]
        for k in range(cw):
            xc = xc + slabs[t + k] * cw_ref[k:k + 1, :]
        xc_scr[t * bsz:(t + 1) * bsz, :] = xc
    a, b = _rglru_coeffs(xc_scr[...], wax_ref, ba_ref[...], bx_ref[...], lam_ref[...], gb)
    h = h0_ref[...]
    for t in range(steps):
        h = a[t * bsz:(t + 1) * bsz, :] * h + b[t * bsz:(t + 1) * bsz, :]
        yg_ref[t] = (h * gate_ref[t]).astype(BF16)
    hlast_ref[...] = h


def _rglru_sample(gate3, rnn3, cs3, h0, cwt, cbias, wax, ba, bx, lam, *, gb):
    steps, bsz, d_rnn = rnn3.shape
    C = _pick(d_rnn, 1024, gb)
    cw = cwt.shape[0]
    vec = pl.BlockSpec((1, C), lambda c: (0, c))
    kern = functools.partial(_rglru_sample_kernel, steps=steps, bsz=bsz, gb=gb, cw=cw)
    return pl.pallas_call(
        kern,
        grid=(d_rnn // C,),
        in_specs=[pl.BlockSpec((steps, bsz, C), lambda c: (0, 0, c)),
                  pl.BlockSpec((steps, bsz, C), lambda c: (0, 0, c)),
                  pl.BlockSpec((cw - 1, bsz, C), lambda c: (0, 0, c)),
                  pl.BlockSpec((bsz, C), lambda c: (0, c)),
                  pl.BlockSpec((cw, C), lambda c: (0, c)),
                  vec,
                  pl.BlockSpec((C // gb, gb, 2 * gb), lambda c: (c, 0, 0)),
                  vec, vec, vec],
        out_specs=[pl.BlockSpec((steps, bsz, C), lambda c: (0, 0, c)),
                   pl.BlockSpec((bsz, C), lambda c: (0, c))],
        out_shape=[jax.ShapeDtypeStruct((steps, bsz, d_rnn), BF16),
                   jax.ShapeDtypeStruct((bsz, d_rnn), F32)],
        scratch_shapes=[pltpu.VMEM((steps * bsz, C), F32)],
        compiler_params=_cp(("arbitrary",)),
        name="rglru_sample",
    )(gate3, rnn3, cs3, h0, cwt, cbias, wax, ba, bx, lam)


def _pool_prompt_kernel(x_ref, pw_ref, pb_ref, ps_ref, g_ref, b_ref, of_ref, xp_scr, z_scr,
                        *, tm, batch, windows, alpha):
    bb = pl.program_id(0)
    i = pl.program_id(1)
    halo = 2 * SUBLANES
    D = x_ref.shape[1]
    pg = D // len(windows)

    @pl.when(bb < batch)
    def _():
        @pl.when(i == 0)
        def _():
            xp_scr[0:halo, :] = jnp.zeros((halo, D), F32)

        xp_scr[halo:halo + tm, :] = x_ref[...]
        pos = i * tm + lax.broadcasted_iota(jnp.int32, (tm, 1), 0)
        for gi, w in enumerate(windows):
            cols = slice(gi * pg, (gi + 1) * pg)
            xg = x_ref[:, cols]
            s = xg
            for j in range(1, w):
                s = s + xp_scr[pl.ds(halo - j, tm), cols]
            cnt = jnp.minimum(w, pos + 1).astype(F32)
            mixed = (s / cnt - xg).astype(BF16)
            out = (_dot(mixed, pw_ref[gi]) + pb_ref[:, cols]) * ps_ref[:, cols]
            z_scr[:, cols] = alpha * xg + out
        of_ref[...] = _layer_norm_rows(z_scr[...], g_ref[...], b_ref[...])
        xp_scr[0:halo, :] = xp_scr[tm:tm + halo, :]

    @pl.when(bb >= batch)
    def _():
        of_ref[...] = jnp.zeros(of_ref.shape, F32)


def _pool_prompt(x, pw, pb, ps, g, b, *, batch, seq, windows, alpha):
    M, D = x.shape
    tm = _pick(math.gcd(seq, M - batch * seq) if M > batch * seq else seq, 256, 16)
    nt = seq // tm
    n_extra = (M - batch * seq) // tm
    if n_extra > nt:
        raise ValueError("extra rows must fit in one pass of the time-tile axis")
    ng = len(windows)

    def rmap(bb, i):
        blk = jnp.where(bb < batch, bb * nt + i, batch * nt + jnp.minimum(i, max(n_extra - 1, 0)))
        return (blk, 0)

    row = pl.BlockSpec((tm, D), rmap)
    vec = pl.BlockSpec((1, D), lambda bb, i: (0, 0))
    kern = functools.partial(_pool_prompt_kernel, tm=tm, batch=batch, windows=windows, alpha=alpha)
    return pl.pallas_call(
        kern,
        grid=(batch + (1 if n_extra else 0), nt),
        in_specs=[row, pl.BlockSpec((ng, D // ng, D // ng), lambda bb, i: (0, 0, 0)), vec, vec, vec, vec],
        out_specs=row,
        out_shape=jax.ShapeDtypeStruct((M, D), F32),
        scratch_shapes=[pltpu.VMEM((tm + 2 * SUBLANES, D), F32), pltpu.VMEM((tm, D), F32)],
        compiler_params=_cp(("arbitrary", "arbitrary")),
        name="pool_prompt",
    )(x, pw, pb, ps, g, b)


def _pool_sample_kernel(st_ref, x_ref, pw_ref, pb_ref, ps_ref, g_ref, b_ref, of_ref,
                        mix_scr, z_scr, *, steps, bt, windows, pos0, alpha):
    nbuf = st_ref.shape[0]
    D = x_ref.shape[2]
    pg = D // len(windows)

    def slab(p, cols):
        return st_ref[p, :, cols] if p < nbuf else x_ref[p - nbuf, :, cols]

    for gi, w in enumerate(windows):
        cols = slice(gi * pg, (gi + 1) * pg)
        for t in range(steps):
            xg = x_ref[t, :, cols]
            s = xg
            for j in range(1, w):
                s = s + slab(nbuf + t - j, cols)
            cnt = float(min(w, pos0 + t + 1))
            mix_scr[t * bt:(t + 1) * bt, cols] = (s / cnt - xg).astype(BF16)
            z_scr[t * bt:(t + 1) * bt, cols] = alpha * xg
    for gi in range(len(windows)):
        cols = slice(gi * pg, (gi + 1) * pg)
        out = (_dot(mix_scr[:, cols], pw_ref[gi]) + pb_ref[:, cols]) * ps_ref[:, cols]
        z_scr[:, cols] = z_scr[:, cols] + out
    o = _layer_norm_rows(z_scr[...], g_ref[...], b_ref[...])
    for t in range(steps):
        of_ref[t] = o[t * bt:(t + 1) * bt, :]


def _pool_sample(st3, x3, pw, pb, ps, g, b, *, windows, pos0, alpha):
    nbuf, bsz, D = st3.shape
    steps = x3.shape[0]
    bt = _pick(bsz, 16, 16)
    ng = len(windows)
    vec = pl.BlockSpec((1, D), lambda j: (0, 0))
    blk = pl.BlockSpec((steps, bt, D), lambda j: (0, j, 0))
    kern = functools.partial(_pool_sample_kernel, steps=steps, bt=bt, windows=windows, pos0=pos0, alpha=alpha)
    return pl.pallas_call(
        kern,
        grid=(bsz // bt,),
        in_specs=[pl.BlockSpec((nbuf, bt, D), lambda j: (0, j, 0)), blk,
                  pl.BlockSpec((ng, D // ng, D // ng), lambda j: (0, 0, 0)), vec, vec, vec, vec],
        out_specs=blk,
        out_shape=jax.ShapeDtypeStruct((steps, bsz, D), F32),
        scratch_shapes=[pltpu.VMEM((steps * bt, D), BF16), pltpu.VMEM((steps * bt, D), F32)],
        compiler_params=_cp(("arbitrary",)),
        name="pool_sample",
    )(st3, x3, pw, pb, ps, g, b)


def _router_kernel(x_ref, wr_ref, sel_ref, e12_ref, w12_ref, *, n_exp):
    x = x_ref[...]
    xh = x.astype(BF16)
    xl = (x - xh.astype(F32)).astype(BF16)
    w = wr_ref[...]
    wh = w.astype(BF16)
    wl = (w - wh.astype(F32)).astype(BF16)
    logits = _dot(xh, wh) + (_dot(xh, wl) + _dot(xl, wh))
    lane = lax.broadcasted_iota(jnp.int32, logits.shape, 1)
    neg = jnp.float32(-jnp.inf)
    logits = jnp.where(lane < n_exp, logits, neg)
    m1 = jnp.max(logits, axis=-1, keepdims=True)
    i1 = jnp.min(jnp.where(logits == m1, lane, LANES), axis=-1, keepdims=True)
    rest = jnp.where(lane == i1, neg, logits)
    m2 = jnp.max(rest, axis=-1, keepdims=True)
    i2 = jnp.min(jnp.where(rest == m2, lane, LANES), axis=-1, keepdims=True)
    e = jnp.exp(m2 - m1)
    den = 1.0 + e
    w1 = 1.0 / den
    w2 = e / den
    sel_ref[...] = jnp.where((lane == i1) | (lane == i2), 1.0, 0.0).astype(F32)
    e12_ref[...] = jnp.where(lane == 0, i1, jnp.where(lane == 1, i2, 0))
    w12_ref[...] = jnp.where(lane == 0, w1, jnp.where(lane == 1, w2, 0.0))


def _router(x, wr_pad, *, n_exp):
    M, D = x.shape
    tm = _pick(M, 512, 16)
    row = pl.BlockSpec((tm, LANES), lambda i: (i, 0))
    return pl.pallas_call(
        functools.partial(_router_kernel, n_exp=n_exp),
        grid=(M // tm,),
        in_specs=[pl.BlockSpec((tm, D), lambda i: (i, 0)), pl.BlockSpec((D, LANES), lambda i: (0, 0))],
        out_specs=[row, row, row],
        out_shape=[jax.ShapeDtypeStruct((M, LANES), F32), jax.ShapeDtypeStruct((M, LANES), jnp.int32),
                   jax.ShapeDtypeStruct((M, LANES), F32)],
        compiler_params=_cp(("arbitrary",)),
        name="moe_router",
    )(x, wr_pad)


def _rank_kernel(sel_ref, rank_ref, cnt_ref, carry):
    i = pl.program_id(0)

    @pl.when(i == 0)
    def _():
        carry[...] = jnp.zeros(carry.shape, F32)

    sel = sel_ref[...]
    tm = sel.shape[0]
    r = lax.broadcasted_iota(jnp.int32, (tm, tm), 0)
    c = lax.broadcasted_iota(jnp.int32, (tm, tm), 1)
    tri = jnp.where(c < r, 1.0, 0.0).astype(BF16)
    excl = _dot(tri, sel.astype(BF16)) + carry[0:1, :]
    rank_ref[...] = excl
    tot = excl[tm - 1:tm, :] + sel[tm - 1:tm, :]
    carry[0:1, :] = tot
    cnt_ref[...] = jnp.broadcast_to(tot, cnt_ref.shape)


def _rank(sel):
    M = sel.shape[0]
    tm = _pick(M, 256, 16)
    return pl.pallas_call(
        _rank_kernel,
        grid=(M // tm,),
        in_specs=[pl.BlockSpec((tm, LANES), lambda i: (i, 0))],
        out_specs=[pl.BlockSpec((tm, LANES), lambda i: (i, 0)), pl.BlockSpec((SUBLANES, LANES), lambda i: (0, 0))],
        out_shape=[jax.ShapeDtypeStruct((M, LANES), F32), jax.ShapeDtypeStruct((SUBLANES, LANES), F32)],
        scratch_shapes=[pltpu.VMEM((SUBLANES, LANES), F32)],
        compiler_params=_cp(("arbitrary",)),
        name="moe_rank",
    )(sel)


def _slot_kernel(rank_ref, e12_ref, base_ref, pos_ref):
    lane = lax.broadcasted_iota(jnp.int32, rank_ref.shape, 1)
    slot = rank_ref[...] + base_ref[...]
    e12 = e12_ref[...]
    p1 = jnp.sum(jnp.where(lane == e12[:, 0:1], slot, 0.0), axis=-1, keepdims=True)
    p2 = jnp.sum(jnp.where(lane == e12[:, 1:2], slot, 0.0), axis=-1, keepdims=True)
    pos_ref[...] = jnp.where(lane == 0, p1, jnp.where(lane == 1, p2, 0.0)).astype(jnp.int32)


def _slots(rank, e12, base_row):
    M = rank.shape[0]
    tm = _pick(M, 512, 16)
    row = pl.BlockSpec((tm, LANES), lambda i: (i, 0))
    return pl.pallas_call(
        _slot_kernel,
        grid=(M // tm,),
        in_specs=[row, row, pl.BlockSpec((1, LANES), lambda i: (0, 0))],
        out_specs=row,
        out_shape=jax.ShapeDtypeStruct((M, LANES), jnp.int32),
        compiler_params=_cp(("arbitrary",)),
        name="moe_slots",
    )(rank, e12, base_row)


def _invert_kernel(p1_ref, p2_ref, src_ref, *, n_tok, n_slot):
    def zero(s, c):
        src_ref[s] = 0
        return c

    lax.fori_loop(0, n_slot, zero, 0, unroll=8)

    def put(t, c):
        src_ref[p1_ref[t]] = t
        src_ref[p2_ref[t]] = t
        return c

    lax.fori_loop(0, n_tok, put, 0, unroll=8)


def _invert(p1, p2, n_slot):
    n_tok = p1.shape[0]
    smem = pl.BlockSpec(memory_space=pltpu.SMEM)
    return pl.pallas_call(
        functools.partial(_invert_kernel, n_tok=n_tok, n_slot=n_slot),
        in_specs=[smem, smem],
        out_specs=smem,
        out_shape=jax.ShapeDtypeStruct((n_slot,), jnp.int32),
        name="moe_invert",
    )(p1, p2)


def _row_copy(src_hbm, row, dst, i, sem):
    return pltpu.make_async_copy(src_hbm.at[pl.ds(row, 1), :], dst.at[pl.ds(i, 1), :], sem)


def _tile_wait(src_hbm, dst, sem):
    pltpu.make_async_copy(src_hbm.at[pl.ds(0, dst.shape[0]), :], dst, sem).wait()


def _dispatch_kernel(valid_ref, src_ref, nsrc_ref, x_hbm, out_ref, buf, sem, *, rows):
    q = pl.program_id(0)
    nq = pl.num_programs(0)

    def start(idx_ref, slot):
        def issue(i, c):
            _row_copy(x_hbm, idx_ref[0, 0, i], buf.at[slot], i, sem.at[slot]).start()
            return c

        lax.fori_loop(0, rows, issue, 0, unroll=8)

    @pl.when((q == 0) & (valid_ref[0] > 0))
    def _():
        start(src_ref, 0)

    nxt = jnp.minimum(q + 1, nq - 1)

    @pl.when((q + 1 < nq) & (valid_ref[nxt] > 0))
    def _():
        start(nsrc_ref, (q + 1) % 2)

    slot = q % 2

    @pl.when(valid_ref[q] > 0)
    def _():
        _tile_wait(x_hbm, buf.at[slot], sem.at[slot])
        out_ref[...] = buf[slot].astype(BF16)

    @pl.when(valid_ref[q] == 0)
    def _():
        out_ref[...] = jnp.zeros(out_ref.shape, BF16)


def _dispatch(x, src, tile_valid):
    n_slot = src.shape[0]
    D = x.shape[1]
    rows = SUB_ROWS
    nt = n_slot // rows
    src3 = src.reshape(nt, 1, rows)
    return pl.pallas_call(
        functools.partial(_dispatch_kernel, rows=rows),
        grid_spec=pltpu.PrefetchScalarGridSpec(
            num_scalar_prefetch=1,
            grid=(nt,),
            in_specs=[pl.BlockSpec((1, 1, rows), lambda q, v: (q, 0, 0), memory_space=pltpu.SMEM),
                      pl.BlockSpec((1, 1, rows), lambda q, v: (jnp.minimum(q + 1, nt - 1), 0, 0),
                                   memory_space=pltpu.SMEM),
                      pl.BlockSpec(memory_space=pl.ANY)],
            out_specs=pl.BlockSpec((rows, D), lambda q, v: (q, 0)),
            scratch_shapes=[pltpu.VMEM((2, rows, D), F32), pltpu.SemaphoreType.DMA((2,))],
        ),
        out_shape=jax.ShapeDtypeStruct((n_slot, D), BF16),
        compiler_params=_cp(("arbitrary",)),
        name="moe_dispatch",
    )(tile_valid, src3, src3, x)


def _combine_kernel(p1_ref, p2_ref, np1_ref, np2_ref, x_ref, w12_ref, g_ref, b_ref, y_hbm, *refs,
                    rows, alpha, bounds):
    n_out = len(bounds)
    out_refs = refs[:n_out]
    ybuf, sem = refs[n_out:]
    i = pl.program_id(0)
    n = pl.num_programs(0)

    def start(a_ref, b_ref_, slot):
        def issue(r, c):
            _row_copy(y_hbm, a_ref[0, 0, r], ybuf.at[slot, 0], r, sem.at[slot]).start()
            _row_copy(y_hbm, b_ref_[0, 0, r], ybuf.at[slot, 1], r, sem.at[slot]).start()
            return c

        lax.fori_loop(0, rows, issue, 0, unroll=8)

    @pl.when(i == 0)
    def _():
        start(p1_ref, p2_ref, 0)

    @pl.when(i + 1 < n)
    def _():
        start(np1_ref, np2_ref, (i + 1) % 2)

    slot = i % 2
    _tile_wait(y_hbm, ybuf.at[slot, 0], sem.at[slot])
    _tile_wait(y_hbm, ybuf.at[slot, 1], sem.at[slot])
    w12 = w12_ref[...]
    ffn = w12[:, 0:1] * ybuf[slot, 0] + w12[:, 1:2] * ybuf[slot, 1]
    res = _layer_norm_rows(alpha * x_ref[...] + ffn, g_ref[...], b_ref[...])
    if n_out == 1:
        out_refs[0][...] = res
    else:
        for (lo, hi), o_ref in zip(bounds, out_refs):
            @pl.when((i >= lo) & (i < hi))
            def _(o_ref=o_ref):
                o_ref[...] = res


def _combine(x, y_sorted, pos1, pos2, w12, g, b, *, alpha, splits):
    M, D = x.shape
    rows = _pick(math.gcd(*splits), SUB_ROWS, 16)
    nt = M // rows
    p1 = pos1.reshape(nt, 1, rows)
    p2 = pos2.reshape(nt, 1, rows)
    idx = pl.BlockSpec((1, 1, rows), lambda i: (i, 0, 0), memory_space=pltpu.SMEM)
    nidx = pl.BlockSpec((1, 1, rows), lambda i: (jnp.minimum(i + 1, nt - 1), 0, 0), memory_space=pltpu.SMEM)
    row = pl.BlockSpec((rows, D), lambda i: (i, 0))
    vec = pl.BlockSpec((1, D), lambda i: (0, 0))
    bounds, out_specs, out_shape = [], [], []
    lo = 0
    for s in splits:
        nb = s // rows
        bounds.append((lo, lo + nb))
        out_specs.append(pl.BlockSpec((rows, D), lambda i, lo=lo, nb=nb: (jnp.clip(i - lo, 0, nb - 1), 0)))
        out_shape.append(jax.ShapeDtypeStruct((s, D), F32))
        lo += nb
    return pl.pallas_call(
        functools.partial(_combine_kernel, rows=rows, alpha=alpha, bounds=tuple(bounds)),
        grid=(nt,),
        in_specs=[idx, idx, nidx, nidx, row, pl.BlockSpec((rows, LANES), lambda i: (i, 0)), vec, vec,
                  pl.BlockSpec(memory_space=pl.ANY)],
        out_specs=out_specs,
        out_shape=out_shape,
        scratch_shapes=[pltpu.VMEM((2, 2, rows, D), F32), pltpu.SemaphoreType.DMA((2,))],
        compiler_params=_cp(("arbitrary",)),
        name="moe_combine",
    )(p1, p2, p1, p2, x, w12, g.reshape(1, D), b.reshape(1, D), y_sorted)


def _piece_loop(nsub, big, fn):
    nbig = lax.div(nsub, jnp.int32(big))

    def big_body(p, c):
        fn(pl.multiple_of(p * (big * SUB_ROWS), big * SUB_ROWS), big * SUB_ROWS)
        return c

    lax.fori_loop(0, nbig, big_body, 0)

    def small_body(s, c):
        fn(pl.multiple_of(s * SUB_ROWS, SUB_ROWS), SUB_ROWS)
        return c

    lax.fori_loop(nbig * big, nsub, small_body, 0)


def _zero_tail(ref, nsub, n_sub):
    def fill(s, c):
        r0 = pl.multiple_of(s * SUB_ROWS, SUB_ROWS)
        ref[pl.ds(r0, SUB_ROWS), :] = jnp.zeros((SUB_ROWS, ref.shape[1]), ref.dtype)
        return c

    lax.fori_loop(nsub, n_sub, fill, 0)


def _moe_up_kernel(exp_ref, nsub_ref, x_ref, wg_ref, wu_ref, h_ref, wgb, wub, *, n_sub, big):
    nsub = nsub_ref[pl.program_id(0)]
    wgb[...] = wg_ref[...].astype(BF16)
    wub[...] = wu_ref[...].astype(BF16)

    def piece(r0, rows):
        x = x_ref[pl.ds(r0, rows), :]
        g = _dot(x, wgb[...])
        u = _dot(x, wub[...])
        h_ref[pl.ds(r0, rows), :] = (g * jax.nn.sigmoid(g) * u).astype(BF16)

    _piece_loop(nsub, big, piece)
    _zero_tail(h_ref, nsub, n_sub)


def _moe_up(xs, wg, wu, n_used, ch_exp, ch_nsub, *, tm_chunk, tf):
    n_slot, D = xs.shape
    F = wg.shape[2]
    tf = _pick(F, tf, LANES)
    n_sub = tm_chunk // SUB_ROWS
    return pl.pallas_call(
        functools.partial(_moe_up_kernel, n_sub=n_sub, big=min(BIG_SUBS, n_sub)),
        grid_spec=pltpu.PrefetchScalarGridSpec(
            num_scalar_prefetch=2,
            grid=(n_used, F // tf),
            in_specs=[pl.BlockSpec((tm_chunk, D), lambda c, f, e, ns: (c, 0), pipeline_mode=pl.Buffered(1)),
                      pl.BlockSpec((None, D, tf), lambda c, f, e, ns: (e[c], 0, f)),
                      pl.BlockSpec((None, D, tf), lambda c, f, e, ns: (e[c], 0, f))],
            out_specs=pl.BlockSpec((tm_chunk, tf), lambda c, f, e, ns: (c, f)),
            scratch_shapes=[pltpu.VMEM((D, tf), BF16), pltpu.VMEM((D, tf), BF16)],
        ),
        out_shape=jax.ShapeDtypeStruct((n_slot, F), BF16),
        compiler_params=_cp(("arbitrary", "arbitrary")),
        name="moe_up",
    )(ch_exp, ch_nsub, xs, wg, wu)


def _moe_down_kernel(exp_ref, nsub_ref, h_ref, wd_ref, y_ref, wdb, *, n_sub, big):
    k = pl.program_id(2)
    nsub = nsub_ref[pl.program_id(0)]
    wdb[...] = wd_ref[...].astype(BF16)

    @pl.when(k == 0)
    def _():
        def first(r0, rows):
            y_ref[pl.ds(r0, rows), :] = _dot(h_ref[pl.ds(r0, rows), :], wdb[...])

        _piece_loop(nsub, big, first)
        _zero_tail(y_ref, nsub, n_sub)

    @pl.when(k > 0)
    def _():
        def acc(r0, rows):
            y_ref[pl.ds(r0, rows), :] += _dot(h_ref[pl.ds(r0, rows), :], wdb[...])

        _piece_loop(nsub, big, acc)


def _moe_down(hs, wd, n_used, ch_exp, ch_nsub, *, tm_chunk, tn, tk):
    n_slot, F = hs.shape
    D = wd.shape[2]
    tn = _pick(D, tn, LANES)
    tk = _pick(F, tk, LANES)
    n_sub = tm_chunk // SUB_ROWS
    return pl.pallas_call(
        functools.partial(_moe_down_kernel, n_sub=n_sub, big=min(BIG_SUBS, n_sub)),
        grid_spec=pltpu.PrefetchScalarGridSpec(
            num_scalar_prefetch=2,
            grid=(n_used, D // tn, F // tk),
            in_specs=[pl.BlockSpec((tm_chunk, tk), lambda c, n, k, e, ns: (c, k)),
                      pl.BlockSpec((None, tk, tn), lambda c, n, k, e, ns: (e[c], k, n))],
            out_specs=pl.BlockSpec((tm_chunk, tn), lambda c, n, k, e, ns: (c, n)),
            scratch_shapes=[pltpu.VMEM((tk, tn), BF16)],
        ),
        out_shape=jax.ShapeDtypeStruct((n_slot, D), F32),
        compiler_params=_cp(("arbitrary", "arbitrary", "arbitrary")),
        name="moe_down",
    )(ch_exp, ch_nsub, hs, wd)


def _chunk_tables(cnt, *, n_exp, tm_chunk, n_chunk):
    cnt = cnt.astype(jnp.int32)
    nch_e = (cnt + tm_chunk - 1) // tm_chunk
    cend = jnp.cumsum(nch_e)
    cstart = cend - nch_e
    total = cend[-1]
    j = jnp.arange(n_chunk, dtype=jnp.int32)
    jj = jnp.minimum(j, total - 1)
    e = jnp.minimum(jnp.sum((jj[:, None] >= cend[None, :]).astype(jnp.int32), axis=1), n_exp - 1)
    local = jj - cstart[e]
    rows = jnp.clip(cnt[e] - local * tm_chunk, 0, tm_chunk)
    nsub = jnp.where(j < total, (rows + SUB_ROWS - 1) // SUB_ROWS, 0).astype(jnp.int32)
    base = (cstart * tm_chunk).astype(F32)
    return total.astype(jnp.int32), e.astype(jnp.int32), nsub, base


def _moe_ffn(x_f32, w_router, wg, wu, wd, ln_g, ln_b, *, alpha, splits):
    M, D = x_f32.shape
    n_exp = wg.shape[0]
    avg = TOP_K * M / n_exp
    tm_chunk = SUB_ROWS * max(1, -(-int(avg * 1.11) // SUB_ROWS))
    n_chunk = n_exp + (TOP_K * M) // tm_chunk
    n_slot = n_chunk * tm_chunk

    wr_pad = jnp.pad(w_router, ((0, 0), (0, LANES - n_exp)))
    sel, e12, w12 = _router(x_f32, wr_pad, n_exp=n_exp)
    rank, cnt = _rank(sel)
    n_used, ch_exp, ch_nsub, base = _chunk_tables(cnt[0, :n_exp], n_exp=n_exp, tm_chunk=tm_chunk,
                                                  n_chunk=n_chunk)
    base_row = jnp.pad(base, (0, LANES - n_exp)).reshape(1, LANES)
    pos = _slots(rank, e12, base_row)
    pos1, pos2 = pos[:, 0], pos[:, 1]
    src = _invert(pos1, pos2, n_slot)
    sub_per_chunk = tm_chunk // SUB_ROWS
    tile_valid = (jnp.arange(sub_per_chunk, dtype=jnp.int32)[None, :] < ch_nsub[:, None])
    tile_valid = tile_valid.astype(jnp.int32).reshape(-1)
    xs = _dispatch(x_f32, src, tile_valid)
    hs = _moe_up(xs, wg, wu, n_used, ch_exp, ch_nsub, tm_chunk=tm_chunk, tf=256)
    ys = _moe_down(hs, wd, n_used, ch_exp, ch_nsub, tm_chunk=tm_chunk, tn=1024, tk=1024)
    return _combine(x_f32, ys, pos1, pos2, w12, ln_g, ln_b, alpha=alpha, splits=splits)


def _last_rows(x2d, batch, seq, n, col0, col1):
    return jnp.stack([lax.slice(x2d, (b * seq + seq - n, col0), ((b + 1) * seq, col1)) for b in range(batch)])


def kernel(x_prompt, x_sample, state_rglru_h, state_rglru_conv, state_pool, rg_w_in, rg_conv_w, rg_conv_b,
           rg_w_a, rg_b_a, rg_w_x, rg_b_x, rg_lambda, rg_w_out, pool_w, pool_b, pool_scale, ffn_w_gate,
           ffn_w_up, ffn_w_down, moe_router, moe_w_gate, moe_w_up, moe_w_down, ln_mix_g, ln_mix_b,
           ln_ffn_g, ln_ffn_b):
    B, S, D = x_prompt.shape
    Bs, Ss, _ = x_sample.shape
    depth = ln_mix_g.shape[0]
    alpha = float((2 * depth) ** 0.25)
    d_rnn = rg_w_a.shape[1] * rg_w_a.shape[2]
    gb = rg_w_a.shape[2]
    n_pool_groups = pool_w.shape[1]
    windows = tuple(2 ** (i + 1) for i in range(n_pool_groups))
    pool_buf = state_pool.shape[2]
    conv_w = rg_conv_w.shape[1]
    Mp, Ms = B * S, Bs * Ss

    xp2 = x_prompt.reshape(Mp, D)
    xs2 = jnp.swapaxes(x_sample, 0, 1).reshape(Ms, D)
    x_parts = [xp2, xs2]
    xb = _cast_rows(x_parts)

    new_h_p, new_conv_p, new_pool_p = [], [], []
    new_h_s, new_conv_s, new_pool_s = [], [], []
    y_parts = None
    for i in range(depth):
        j = i // 2
        last = i == depth - 1
        if i % 2 == 0:
            u = _mm_fullk([xb], [rg_w_in[j]], kind="gelu_split", out_dtype=F32, tm=1024, tn=512,
                          n_gelu_cols=d_rnn, name="rg_in_proj")
            wax = jnp.concatenate([rg_w_a[j], rg_w_x[j]], axis=-1).astype(BF16)
            vec = lambda v: v.reshape(1, d_rnn)
            args = (rg_conv_w[j], vec(rg_conv_b[j]), wax, vec(rg_b_a[j]), vec(rg_b_x[j]), vec(rg_lambda[j]))
            yg_p, h_p = _rglru_prompt(u, *args, batch=B, seq=S, d_rnn=d_rnn, gb=gb)
            u_s = lax.slice(u, (Mp, 0), (Mp + Ms, 2 * d_rnn)).reshape(Ss, Bs, 2 * d_rnn)
            cs3 = jnp.swapaxes(state_rglru_conv[j], 0, 1)
            rnn_s = u_s[:, :, d_rnn:]
            yg_s, h_s = _rglru_sample(u_s[:, :, :d_rnn], rnn_s, cs3, state_rglru_h[j], *args, gb=gb)
            new_h_p.append(h_p)
            new_h_s.append(h_s)
            new_conv_p.append(_last_rows(u, B, S, conv_w - 1, d_rnn, 2 * d_rnn))
            new_conv_s.append(jnp.swapaxes(jnp.concatenate([cs3, rnn_s], axis=0)[-(conv_w - 1):], 0, 1))
            mix = _mm_fullk([yg_p, yg_s.reshape(Ms, d_rnn)], [rg_w_out[j]], kind="plain", out_dtype=F32,
                            tm=1024, tn=512, name="rg_out_proj")
            x, xb = _deepnorm(x_parts, mix, ln_mix_g[i], ln_mix_b[i], alpha=alpha, want_bf16=True, name="ln_mix")
            hid = _mm_fullk([xb], [ffn_w_gate[j], ffn_w_up[j]], kind="swiglu", out_dtype=BF16, tm=512, tn=512,
                            name="ffn_up")
            ffn = _mm_ktiled(hid, ffn_w_down[j], tm=2304, tn=1024, tk=1024, name="ffn_down")
            x, xb = _deepnorm([x], ffn, ln_ffn_g[i], ln_ffn_b[i], alpha=alpha, want_bf16=not last, name="ln_ffn")
            x_parts = [x]
        else:
            if len(x_parts) != 1:
                x = jnp.concatenate(x_parts, axis=0)
            pw = pool_w[j].astype(BF16)
            vec = lambda v: v.reshape(1, D)
            pargs = (pw, vec(pool_b[j]), vec(pool_scale[j]), vec(ln_mix_g[i]), vec(ln_mix_b[i]))
            x_s3 = lax.slice(x, (Mp, 0), (Mp + Ms, D)).reshape(Ss, Bs, D)
            st3 = jnp.swapaxes(state_pool[j], 0, 1)
            xm = _pool_prompt(x, *pargs, batch=B, seq=S, windows=windows, alpha=alpha)
            xs_f = _pool_sample(st3, x_s3, *pargs, windows=windows, pos0=PAST_LEN, alpha=alpha)
            new_pool_p.append(_last_rows(x, B, S, pool_buf, 0, D))
            new_pool_s.append(jnp.swapaxes(jnp.concatenate([st3, x_s3], axis=0)[-pool_buf:], 0, 1))
            xm = lax.dynamic_update_slice(xm, xs_f.reshape(Ms, D), (Mp, 0))
            outs = _moe_ffn(xm, moe_router[j], moe_w_gate[j], moe_w_up[j], moe_w_down[j],
                            ln_ffn_g[i], ln_ffn_b[i], alpha=alpha, splits=(Mp, Ms))
            x_parts = list(outs)
            if not last:
                xb = _cast_rows(x_parts)

    if len(x_parts) == 1:
        x_parts = [x_parts[0][:Mp], x_parts[0][Mp:]]
    y_prompt = x_parts[0].reshape(B, S, D)
    y_sample = jnp.swapaxes(x_parts[1].reshape(Ss, Bs, D), 0, 1)
    return (y_prompt, y_sample, jnp.stack(new_h_p), jnp.stack(new_conv_p), jnp.stack(new_pool_p),
            jnp.stack(new_h_s), jnp.stack(new_conv_s), jnp.stack(new_pool_s))
```

```python
import functools
import math

import jax
import jax.numpy as jnp
from jax import lax
from jax.experimental import pallas as pl
from jax.experimental.pallas import tpu as pltpu

F32 = jnp.float32
BF16 = jnp.bfloat16

LRU_C = 8.0
LN_EPS = 1e-5
PAST_LEN = 16384
TOP_K = 2
GELU_C = 0.7978845608028654

LANES = 128
SUBLANES = 8
SUB_ROWS = 256
BIG_SUBS = 4
VMEM_LIMIT = 56 * 1024 * 1024


def _pick(n, target, mult):
    best = None
    d = mult
    while d <= min(n, target):
        if n % d == 0:
            best = d
        d += mult
    return best if best is not None else n


def _cp(sem):
    return pltpu.CompilerParams(dimension_semantics=sem, vmem_limit_bytes=VMEM_LIMIT)


def _dot(a, b):
    return jnp.dot(a, b, preferred_element_type=F32)


def _gelu_tanh(x):
    return 0.5 * x * (1.0 + jnp.tanh(GELU_C * (x + 0.044715 * (x * x * x))))


def _softplus(z):
    return jnp.log1p(jnp.exp(-jnp.abs(z))) + jnp.maximum(z, 0.0)


def _sigmoid_tanh(x):
    return 0.5 + 0.5 * jnp.tanh(0.5 * x)


def _layer_norm_rows(z, g, b):
    mu = jnp.mean(z, axis=-1, keepdims=True)
    zc = z - mu
    var = jnp.mean(zc * zc, axis=-1, keepdims=True)
    return zc * lax.rsqrt(var + LN_EPS) * g + b


def _row_sources(arrs, tm, row_axis):
    specs, ranges = [], []
    lo = 0
    for a in arrs:
        nb = a.shape[0] // tm
        if nb * tm != a.shape[0]:
            raise ValueError("row source not divisible by the row tile")

        def imap(*idx, lo=lo, nb=nb):
            return (jnp.clip(idx[row_axis] - lo, 0, nb - 1), 0)

        specs.append(pl.BlockSpec((tm, a.shape[1]), imap))
        ranges.append((lo, lo + nb))
        lo += nb
    return specs, ranges, lo


def _for_row_source(m, ranges, refs, body):
    if len(refs) == 1:
        body(refs[0])
        return
    for (lo, hi), r in zip(ranges, refs):
        @pl.when((m >= lo) & (m < hi))
        def _(r=r):
            body(r)


def _common_tile(arrs, target, mult):
    return _pick(math.gcd(*[a.shape[0] for a in arrs]), target, mult)


def _cast_kernel(*refs, n_x, ranges):
    o_ref = refs[n_x]

    def body(x_ref):
        o_ref[...] = x_ref[...].astype(BF16)

    _for_row_source(pl.program_id(0), ranges, refs[:n_x], body)


def _cast_rows(xs):
    D = xs[0].shape[1]
    tm = _common_tile(xs, 512, 16)
    x_specs, ranges, n_m = _row_sources(xs, tm, row_axis=0)
    return pl.pallas_call(
        functools.partial(_cast_kernel, n_x=len(xs), ranges=ranges),
        grid=(n_m,),
        in_specs=x_specs,
        out_specs=pl.BlockSpec((tm, D), lambda i: (i, 0)),
        out_shape=jax.ShapeDtypeStruct((n_m * tm, D), BF16),
        compiler_params=_cp(("arbitrary",)),
        name="cast_rows",
    )(*xs)


def _mm_fullk_kernel(*refs, kind, n_x, n_w, ranges, n_gelu_tiles):
    x_refs = refs[:n_x]
    w_refs = refs[n_x:n_x + n_w]
    out_ref = refs[n_x + n_w]
    n = pl.program_id(0)
    m = pl.program_id(1)

    def body(x_ref):
        x = x_ref[...]
        if kind == "swiglu":
            g = _dot(x, w_refs[0][...].astype(BF16))
            u = _dot(x, w_refs[1][...].astype(BF16))
            out_ref[...] = (g * jax.nn.sigmoid(g) * u).astype(out_ref.dtype)
        elif kind == "gelu_split":
            u = _dot(x, w_refs[0][...].astype(BF16))

            @pl.when(n < n_gelu_tiles)
            def _():
                out_ref[...] = _gelu_tanh(u)

            @pl.when(n >= n_gelu_tiles)
            def _():
                out_ref[...] = u
        else:
            out_ref[...] = _dot(x, w_refs[0][...].astype(BF16)).astype(out_ref.dtype)

    _for_row_source(m, ranges, x_refs, body)


def _mm_fullk(xs, ws, *, kind, out_dtype, tm, tn, n_gelu_cols=0, name):
    K = xs[0].shape[1]
    N = ws[0].shape[1]
    tm = _common_tile(xs, tm, 16)
    tn = _pick(N, tn, LANES)
    x_specs, ranges, n_m = _row_sources(xs, tm, row_axis=1)
    kern = functools.partial(_mm_fullk_kernel, kind=kind, n_x=len(xs), n_w=len(ws), ranges=ranges,
                             n_gelu_tiles=n_gelu_cols // tn)
    return pl.pallas_call(
        kern,
        grid=(N // tn, n_m),
        in_specs=x_specs + [pl.BlockSpec((K, tn), lambda n, m: (0, n)) for _ in ws],
        out_specs=pl.BlockSpec((tm, tn), lambda n, m: (m, n)),
        out_shape=jax.ShapeDtypeStruct((n_m * tm, N), out_dtype),
        compiler_params=_cp(("arbitrary", "arbitrary")),
        name=name,
    )(*xs, *ws)


def _mm_ktiled_kernel(x_ref, w_ref, o_ref, *, n_piece):
    k = pl.program_id(2)
    rows = x_ref.shape[0] // n_piece

    @pl.when(k == 0)
    def _():
        for p in range(n_piece):
            o_ref[p * rows:(p + 1) * rows, :] = _dot(x_ref[p * rows:(p + 1) * rows, :], w_ref[...].astype(BF16))

    @pl.when(k > 0)
    def _():
        for p in range(n_piece):
            o_ref[p * rows:(p + 1) * rows, :] += _dot(x_ref[p * rows:(p + 1) * rows, :], w_ref[...].astype(BF16))


def _mm_ktiled(x, w, *, tm, tn, tk, name):
    M, K = x.shape
    N = w.shape[1]
    tm = _pick(M, tm, 16)
    tn = _pick(N, tn, LANES)
    tk = _pick(K, tk, LANES)
    n_piece = 3 if tm % (3 * 16) == 0 else 1
    return pl.pallas_call(
        functools.partial(_mm_ktiled_kernel, n_piece=n_piece),
        grid=(M // tm, N // tn, K // tk),
        in_specs=[pl.BlockSpec((tm, tk), lambda m, n, k: (m, k)),
                  pl.BlockSpec((tk, tn), lambda m, n, k: (k, n))],
        out_specs=pl.BlockSpec((tm, tn), lambda m, n, k: (m, n)),
        out_shape=jax.ShapeDtypeStruct((M, N), F32),
        compiler_params=_cp(("arbitrary", "arbitrary", "arbitrary")),
        name=name,
    )(x, w)


def _ln_kernel(*refs, alpha, n_x, ranges, want_bf16):
    x_refs = refs[:n_x]
    y_ref, g_ref, b_ref, of_ref = refs[n_x:n_x + 4]
    i = pl.program_id(0)

    def body(x_ref):
        o = _layer_norm_rows(alpha * x_ref[...] + y_ref[...], g_ref[...], b_ref[...])
        of_ref[...] = o
        if want_bf16:
            refs[n_x + 4][...] = o.astype(BF16)

    _for_row_source(i, ranges, x_refs, body)


def _deepnorm(xs, y, g, b, *, alpha, want_bf16, name):
    M, D = y.shape
    tm = _common_tile(xs, 256, 16)
    x_specs, ranges, n_m = _row_sources(xs, tm, row_axis=0)
    row = pl.BlockSpec((tm, D), lambda i: (i, 0))
    vec = pl.BlockSpec((1, D), lambda i: (0, 0))
    out_specs = [row, row] if want_bf16 else [row]
    out_shape = [jax.ShapeDtypeStruct((M, D), F32)] + ([jax.ShapeDtypeStruct((M, D), BF16)] if want_bf16 else [])
    outs = pl.pallas_call(
        functools.partial(_ln_kernel, alpha=alpha, n_x=len(xs), ranges=ranges, want_bf16=want_bf16),
        grid=(n_m,),
        in_specs=x_specs + [row, vec, vec],
        out_specs=out_specs,
        out_shape=out_shape,
        compiler_params=_cp(("arbitrary",)),
        name=name,
    )(*xs, y, g.reshape(1, D), b.reshape(1, D))
    return (outs[0], outs[1]) if want_bf16 else (outs[0], None)


def _rglru_coeffs(xc, wax_ref, ba, bx, lam, gb):
    xcb = xc.astype(BF16)
    nblk = xc.shape[1] // gb
    r_parts, i_parts = [], []
    for j in range(nblk):
        ru = _dot(xcb[:, j * gb:(j + 1) * gb], wax_ref[j])
        r_parts.append(ru[:, :gb])
        i_parts.append(ru[:, gb:])
    r = _sigmoid_tanh(jnp.concatenate(r_parts, axis=1) + ba)
    ig = _sigmoid_tanh(jnp.concatenate(i_parts, axis=1) + bx)
    log_a = (-LRU_C * r) * _softplus(-lam)
    t = jnp.tanh(0.5 * log_a)
    q = 1.0 / (1.0 - t)
    a = (1.0 + t) * q
    one_minus_a2 = (-4.0 * t) * (q * q)
    b = jnp.sqrt(one_minus_a2) * (ig * xc)
    return a, b


def _rglru_prompt_kernel(gate_ref, rnn_ref, cw_ref, cb_ref, wax_ref, ba_ref, bx_ref, lam_ref,
                         yg_ref, hlast_ref, xp_scr, h_scr, a_scr, b_scr, *, tc, gb, cw):
    b_idx = pl.program_id(1)
    i = pl.program_id(2)
    nt = pl.num_programs(2)
    halo = SUBLANES

    @pl.when(i == 0)
    def _():
        xp_scr[0:halo, :] = jnp.zeros((halo, xp_scr.shape[1]), F32)
        h_scr[...] = jnp.zeros(h_scr.shape, F32)

    xp_scr[halo:halo + tc, :] = rnn_ref[...]
    xc = cb_ref[...]
    for k in range(cw):
        xc = xc + xp_scr[pl.ds(halo - (cw - 1) + k, tc), :] * cw_ref[k:k + 1, :]
    a, b = _rglru_coeffs(xc, wax_ref, ba_ref[...], bx_ref[...], lam_ref[...], gb)
    a_scr[...] = a
    b_scr[...] = b

    def step(t, h):
        h = a_scr[pl.ds(t, 1), :] * h + b_scr[pl.ds(t, 1), :]
        b_scr[pl.ds(t, 1), :] = h
        return h

    h = lax.fori_loop(0, tc, step, h_scr[0:1, :], unroll=8)
    h_scr[0:1, :] = h
    yg_ref[...] = (b_scr[...] * gate_ref[...]).astype(BF16)
    xp_scr[0:halo, :] = xp_scr[tc:tc + halo, :]

    @pl.when(i == nt - 1)
    def _():
        hlast_ref[pl.ds(b_idx, 1), :] = h


def _rglru_prompt(u, cwt, cbias, wax, ba, bx, lam, *, batch, seq, d_rnn, gb):
    C = _pick(d_rnn, 1024, gb)
    tc = _pick(seq, 256, SUBLANES)
    nt = seq // tc
    nc = d_rnn // C
    cw = cwt.shape[0]
    vec = pl.BlockSpec((1, C), lambda c, b, i: (0, c))
    kern = functools.partial(_rglru_prompt_kernel, tc=tc, gb=gb, cw=cw)
    return pl.pallas_call(
        kern,
        grid=(nc, batch, nt),
        in_specs=[pl.BlockSpec((tc, C), lambda c, b, i: (b * nt + i, c)),
                  pl.BlockSpec((tc, C), lambda c, b, i: (b * nt + i, nc + c)),
                  pl.BlockSpec((cw, C), lambda c, b, i: (0, c)),
                  vec,
                  pl.BlockSpec((C // gb, gb, 2 * gb), lambda c, b, i: (c, 0, 0)),
                  vec, vec, vec],
        out_specs=[pl.BlockSpec((tc, C), lambda c, b, i: (b * nt + i, c)),
                   pl.BlockSpec((batch, C), lambda c, b, i: (0, c))],
        out_shape=[jax.ShapeDtypeStruct((batch * seq, d_rnn), BF16),
                   jax.ShapeDtypeStruct((batch, d_rnn), F32)],
        scratch_shapes=[pltpu.VMEM((tc + SUBLANES, C), F32), pltpu.VMEM((SUBLANES, C), F32),
                        pltpu.VMEM((tc, C), F32), pltpu.VMEM((tc, C), F32)],
        compiler_params=_cp(("arbitrary", "arbitrary", "arbitrary")),
        name="rglru_prompt",
    )(u, u, cwt, cbias, wax, ba, bx, lam)


def _rglru_sample_kernel(gate_ref, rnn_ref, cs_ref, h0_ref, cw_ref, cb_ref, wax_ref, ba_ref, bx_ref,
                         lam_ref, yg_ref, hlast_ref, xc_scr, *, steps, bsz, gb, cw):
    slabs = [cs_ref[k] for k in range(cw - 1)] + [rnn_ref[t] for t in range(steps)]
    for t in range(steps):
        xc = cb_ref[...]
        for k in range(cw):
            xc = xc + slabs[t + k] * cw_ref[k:k + 1, :]
        xc_scr[t * bsz:(t + 1) * bsz, :] = xc
    a, b = _rglru_coeffs(xc_scr[...], wax_ref, ba_ref[...], bx_ref[...], lam_ref[...], gb)
    h = h0_ref[...]
    for t in range(steps):
        h = a[t * bsz:(t + 1) * bsz, :] * h + b[t * bsz:(t + 1) * bsz, :]
        yg_ref[t] = (h * gate_ref[t]).astype(BF16)
    hlast_ref[...] = h


def _rglru_sample(gate3, rnn3, cs3, h0, cwt, cbias, wax, ba, bx, lam, *, gb):
    steps, bsz, d_rnn = rnn3.shape
    C = _pick(d_rnn, 1024, gb)
    cw = cwt.shape[0]
    vec = pl.BlockSpec((1, C), lambda c: (0, c))
    kern = functools.partial(_rglru_sample_kernel, steps=steps, bsz=bsz, gb=gb, cw=cw)
    return pl.pallas_call(
        kern,
        grid=(d_rnn // C,),
        in_specs=[pl.BlockSpec((steps, bsz, C), lambda c: (0, 0, c)),
                  pl.BlockSpec((steps, bsz, C), lambda c: (0, 0, c)),
                  pl.BlockSpec((cw - 1, bsz, C), lambda c: (0, 0, c)),
                  pl.BlockSpec((bsz, C), lambda c: (0, c)),
                  pl.BlockSpec((cw, C), lambda c: (0, c)),
                  vec,
                  pl.BlockSpec((C // gb, gb, 2 * gb), lambda c: (c, 0, 0)),
                  vec, vec, vec],
        out_specs=[pl.BlockSpec((steps, bsz, C), lambda c: (0, 0, c)),
                   pl.BlockSpec((bsz, C), lambda c: (0, c))],
        out_shape=[jax.ShapeDtypeStruct((steps, bsz, d_rnn), BF16),
                   jax.ShapeDtypeStruct((bsz, d_rnn), F32)],
        scratch_shapes=[pltpu.VMEM((steps * bsz, C), F32)],
        compiler_params=_cp(("arbitrary",)),
        name="rglru_sample",
    )(gate3, rnn3, cs3, h0, cwt, cbias, wax, ba, bx, lam)


def _pool_prompt_kernel(x_ref, pw_ref, pb_ref, ps_ref, g_ref, b_ref, of_ref, xp_scr, z_scr,
                        *, tm, batch, windows, alpha):
    bb = pl.program_id(0)
    i = pl.program_id(1)
    halo = 2 * SUBLANES
    D = x_ref.shape[1]
    pg = D // len(windows)

    @pl.when(bb < batch)
    def _():
        @pl.when(i == 0)
        def _():
            xp_scr[0:halo, :] = jnp.zeros((halo, D), F32)

        xp_scr[halo:halo + tm, :] = x_ref[...]
        pos = i * tm + lax.broadcasted_iota(jnp.int32, (tm, 1), 0)
        for gi, w in enumerate(windows):
            cols = slice(gi * pg, (gi + 1) * pg)
            xg = x_ref[:, cols]
            s = xg
            for j in range(1, w):
                s = s + xp_scr[pl.ds(halo - j, tm), cols]
            cnt = jnp.minimum(w, pos + 1).astype(F32)
            mixed = (s / cnt - xg).astype(BF16)
            out = (_dot(mixed, pw_ref[gi]) + pb_ref[:, cols]) * ps_ref[:, cols]
            z_scr[:, cols] = alpha * xg + out
        of_ref[...] = _layer_norm_rows(z_scr[...], g_ref[...], b_ref[...])
        xp_scr[0:halo, :] = xp_scr[tm:tm + halo, :]

    @pl.when(bb >= batch)
    def _():
        of_ref[...] = jnp.zeros(of_ref.shape, F32)


def _pool_prompt(x, pw, pb, ps, g, b, *, batch, seq, windows, alpha):
    M, D = x.shape
    tm = _pick(math.gcd(seq, M - batch * seq) if M > batch * seq else seq, 256, 16)
    nt = seq // tm
    n_extra = (M - batch * seq) // tm
    if n_extra > nt:
        raise ValueError("extra rows must fit in one pass of the time-tile axis")
    ng = len(windows)

    def rmap(bb, i):
        blk = jnp.where(bb < batch, bb * nt + i, batch * nt + jnp.minimum(i, max(n_extra - 1, 0)))
        return (blk, 0)

    row = pl.BlockSpec((tm, D), rmap)
    vec = pl.BlockSpec((1, D), lambda bb, i: (0, 0))
    kern = functools.partial(_pool_prompt_kernel, tm=tm, batch=batch, windows=windows, alpha=alpha)
    return pl.pallas_call(
        kern,
        grid=(batch + (1 if n_extra else 0), nt),
        in_specs=[row, pl.BlockSpec((ng, D // ng, D // ng), lambda bb, i: (0, 0, 0)), vec, vec, vec, vec],
        out_specs=row,
        out_shape=jax.ShapeDtypeStruct((M, D), F32),
        scratch_shapes=[pltpu.VMEM((tm + 2 * SUBLANES, D), F32), pltpu.VMEM((tm, D), F32)],
        compiler_params=_cp(("arbitrary", "arbitrary")),
        name="pool_prompt",
    )(x, pw, pb, ps, g, b)


def _pool_sample_kernel(st_ref, x_ref, pw_ref, pb_ref, ps_ref, g_ref, b_ref, of_ref,
                        mix_scr, z_scr, *, steps, bt, windows, pos0, alpha):
    nbuf = st_ref.shape[0]
    D = x_ref.shape[2]
    pg = D // len(windows)

    def slab(p, cols):
        return st_ref[p, :, cols] if p < nbuf else x_ref[p - nbuf, :, cols]

    for gi, w in enumerate(windows):
        cols = slice(gi * pg, (gi + 1) * pg)
        for t in range(steps):
            xg = x_ref[t, :, cols]
            s = xg
            for j in range(1, w):
                s = s + slab(nbuf + t - j, cols)
            cnt = float(min(w, pos0 + t + 1))
            mix_scr[t * bt:(t + 1) * bt, cols] = (s / cnt - xg).astype(BF16)
            z_scr[t * bt:(t + 1) * bt, cols] = alpha * xg
    for gi in range(len(windows)):
        cols = slice(gi * pg, (gi + 1) * pg)
        out = (_dot(mix_scr[:, cols], pw_ref[gi]) + pb_ref[:, cols]) * ps_ref[:, cols]
        z_scr[:, cols] = z_scr[:, cols] + out
    o = _layer_norm_rows(z_scr[...], g_ref[...], b_ref[...])
    for t in range(steps):
        of_ref[t] = o[t * bt:(t + 1) * bt, :]


def _pool_sample(st3, x3, pw, pb, ps, g, b, *, windows, pos0, alpha):
    nbuf, bsz, D = st3.shape
    steps = x3.shape[0]
    bt = _pick(bsz, 16, 16)
    ng = len(windows)
    vec = pl.BlockSpec((1, D), lambda j: (0, 0))
    blk = pl.BlockSpec((steps, bt, D), lambda j: (0, j, 0))
    kern = functools.partial(_pool_sample_kernel, steps=steps, bt=bt, windows=windows, pos0=pos0, alpha=alpha)
    return pl.pallas_call(
        kern,
        grid=(bsz // bt,),
        in_specs=[pl.BlockSpec((nbuf, bt, D), lambda j: (0, j, 0)), blk,
                  pl.BlockSpec((ng, D // ng, D // ng), lambda j: (0, 0, 0)), vec, vec, vec, vec],
        out_specs=blk,
        out_shape=jax.ShapeDtypeStruct((steps, bsz, D), F32),
        scratch_shapes=[pltpu.VMEM((steps * bt, D), BF16), pltpu.VMEM((steps * bt, D), F32)],
        compiler_params=_cp(("arbitrary",)),
        name="pool_sample",
    )(st3, x3, pw, pb, ps, g, b)


def _router_kernel(x_ref, wr_ref, sel_ref, e12_ref, w12_ref, *, n_exp):
    x = x_ref[...]
    xh = x.astype(BF16)
    xl = (x - xh.astype(F32)).astype(BF16)
    w = wr_ref[...]
    wh = w.astype(BF16)
    wl = (w - wh.astype(F32)).astype(BF16)
    logits = _dot(xh, wh) + (_dot(xh, wl) + _dot(xl, wh))
    lane = lax.broadcasted_iota(jnp.int32, logits.shape, 1)
    neg = jnp.float32(-jnp.inf)
    logits = jnp.where(lane < n_exp, logits, neg)
    m1 = jnp.max(logits, axis=-1, keepdims=True)
    i1 = jnp.min(jnp.where(logits == m1, lane, LANES), axis=-1, keepdims=True)
    rest = jnp.where(lane == i1, neg, logits)
    m2 = jnp.max(rest, axis=-1, keepdims=True)
    i2 = jnp.min(jnp.where(rest == m2, lane, LANES), axis=-1, keepdims=True)
    e = jnp.exp(m2 - m1)
    den = 1.0 + e
    w1 = 1.0 / den
    w2 = e / den
    sel_ref[...] = jnp.where((lane == i1) | (lane == i2), 1.0, 0.0).astype(F32)
    e12_ref[...] = jnp.where(lane == 0, i1, jnp.where(lane == 1, i2, 0))
    w12_ref[...] = jnp.where(lane == 0, w1, jnp.where(lane == 1, w2, 0.0))


def _router(x, wr_pad, *, n_exp):
    M, D = x.shape
    tm = _pick(M, 512, 16)
    row = pl.BlockSpec((tm, LANES), lambda i: (i, 0))
    return pl.pallas_call(
        functools.partial(_router_kernel, n_exp=n_exp),
        grid=(M // tm,),
        in_specs=[pl.BlockSpec((tm, D), lambda i: (i, 0)), pl.BlockSpec((D, LANES), lambda i: (0, 0))],
        out_specs=[row, row, row],
        out_shape=[jax.ShapeDtypeStruct((M, LANES), F32), jax.ShapeDtypeStruct((M, LANES), jnp.int32),
                   jax.ShapeDtypeStruct((M, LANES), F32)],
        compiler_params=_cp(("arbitrary",)),
        name="moe_router",
    )(x, wr_pad)


def _rank_kernel(sel_ref, rank_ref, cnt_ref, carry):
    i = pl.program_id(0)

    @pl.when(i == 0)
    def _():
        carry[...] = jnp.zeros(carry.shape, F32)

    sel = sel_ref[...]
    tm = sel.shape[0]
    r = lax.broadcasted_iota(jnp.int32, (tm, tm), 0)
    c = lax.broadcasted_iota(jnp.int32, (tm, tm), 1)
    tri = jnp.where(c < r, 1.0, 0.0).astype(BF16)
    excl = _dot(tri, sel.astype(BF16)) + carry[0:1, :]
    rank_ref[...] = excl
    tot = excl[tm - 1:tm, :] + sel[tm - 1:tm, :]
    carry[0:1, :] = tot
    cnt_ref[...] = jnp.broadcast_to(tot, cnt_ref.shape)


def _rank(sel):
    M = sel.shape[0]
    tm = _pick(M, 256, 16)
    return pl.pallas_call(
        _rank_kernel,
        grid=(M // tm,),
        in_specs=[pl.BlockSpec((tm, LANES), lambda i: (i, 0))],
        out_specs=[pl.BlockSpec((tm, LANES), lambda i: (i, 0)), pl.BlockSpec((SUBLANES, LANES), lambda i: (0, 0))],
        out_shape=[jax.ShapeDtypeStruct((M, LANES), F32), jax.ShapeDtypeStruct((SUBLANES, LANES), F32)],
        scratch_shapes=[pltpu.VMEM((SUBLANES, LANES), F32)],
        compiler_params=_cp(("arbitrary",)),
        name="moe_rank",
    )(sel)


def _slot_kernel(rank_ref, e12_ref, base_ref, pos_ref):
    lane = lax.broadcasted_iota(jnp.int32, rank_ref.shape, 1)
    slot = rank_ref[...] + base_ref[...]
    e12 = e12_ref[...]
    p1 = jnp.sum(jnp.where(lane == e12[:, 0:1], slot, 0.0), axis=-1, keepdims=True)
    p2 = jnp.sum(jnp.where(lane == e12[:, 1:2], slot, 0.0), axis=-1, keepdims=True)
    pos_ref[...] = jnp.where(lane == 0, p1, jnp.where(lane == 1, p2, 0.0)).astype(jnp.int32)


def _slots(rank, e12, base_row):
    M = rank.shape[0]
    tm = _pick(M, 512, 16)
    row = pl.BlockSpec((tm, LANES), lambda i: (i, 0))
    return pl.pallas_call(
        _slot_kernel,
        grid=(M // tm,),
        in_specs=[row, row, pl.BlockSpec((1, LANES), lambda i: (0, 0))],
        out_specs=row,
        out_shape=jax.ShapeDtypeStruct((M, LANES), jnp.int32),
        compiler_params=_cp(("arbitrary",)),
        name="moe_slots",
    )(rank, e12, base_row)


def _invert_kernel(p1_ref, p2_ref, src_ref, *, n_tok, n_slot):
    def zero(s, c):
        src_ref[s] = 0
        return c

    lax.fori_loop(0, n_slot, zero, 0, unroll=8)

    def put(t, c):
        src_ref[p1_ref[t]] = t
        src_ref[p2_ref[t]] = t
        return c

    lax.fori_loop(0, n_tok, put, 0, unroll=8)


def _invert(p1, p2, n_slot):
    n_tok = p1.shape[0]
    smem = pl.BlockSpec(memory_space=pltpu.SMEM)
    return pl.pallas_call(
        functools.partial(_invert_kernel, n_tok=n_tok, n_slot=n_slot),
        in_specs=[smem, smem],
        out_specs=smem,
        out_shape=jax.ShapeDtypeStruct((n_slot,), jnp.int32),
        name="moe_invert",
    )(p1, p2)


def _row_copy(src_hbm, row, dst, i, sem):
    return pltpu.make_async_copy(src_hbm.at[pl.ds(row, 1), :], dst.at[pl.ds(i, 1), :], sem)


def _tile_wait(src_hbm, dst, sem):
    pltpu.make_async_copy(src_hbm.at[pl.ds(0, dst.shape[0]), :], dst, sem).wait()


def _dispatch_kernel(valid_ref, src_ref, nsrc_ref, x_hbm, out_ref, buf, sem, *, rows):
    q = pl.program_id(0)
    nq = pl.num_programs(0)

    def start(idx_ref, slot):
        def issue(i, c):
            _row_copy(x_hbm, idx_ref[0, 0, i], buf.at[slot], i, sem.at[slot]).start()
            return c

        lax.fori_loop(0, rows, issue, 0, unroll=8)

    @pl.when((q == 0) & (valid_ref[0] > 0))
    def _():
        start(src_ref, 0)

    nxt = jnp.minimum(q + 1, nq - 1)

    @pl.when((q + 1 < nq) & (valid_ref[nxt] > 0))
    def _():
        start(nsrc_ref, (q + 1) % 2)

    slot = q % 2

    @pl.when(valid_ref[q] > 0)
    def _():
        _tile_wait(x_hbm, buf.at[slot], sem.at[slot])
        out_ref[...] = buf[slot].astype(BF16)

    @pl.when(valid_ref[q] == 0)
    def _():
        out_ref[...] = jnp.zeros(out_ref.shape, BF16)


def _dispatch(x, src, tile_valid):
    n_slot = src.shape[0]
    D = x.shape[1]
    rows = SUB_ROWS
    nt = n_slot // rows
    src3 = src.reshape(nt, 1, rows)
    return pl.pallas_call(
        functools.partial(_dispatch_kernel, rows=rows),
        grid_spec=pltpu.PrefetchScalarGridSpec(
            num_scalar_prefetch=1,
            grid=(nt,),
            in_specs=[pl.BlockSpec((1, 1, rows), lambda q, v: (q, 0, 0), memory_space=pltpu.SMEM),
                      pl.BlockSpec((1, 1, rows), lambda q, v: (jnp.minimum(q + 1, nt - 1), 0, 0),
                                   memory_space=pltpu.SMEM),
                      pl.BlockSpec(memory_space=pl.ANY)],
            out_specs=pl.BlockSpec((rows, D), lambda q, v: (q, 0)),
            scratch_shapes=[pltpu.VMEM((2, rows, D), F32), pltpu.SemaphoreType.DMA((2,))],
        ),
        out_shape=jax.ShapeDtypeStruct((n_slot, D), BF16),
        compiler_params=_cp(("arbitrary",)),
        name="moe_dispatch",
    )(tile_valid, src3, src3, x)


def _combine_kernel(p1_ref, p2_ref, np1_ref, np2_ref, x_ref, w12_ref, g_ref, b_ref, y_hbm, *refs,
                    rows, alpha, bounds):
    n_out = len(bounds)
    out_refs = refs[:n_out]
    ybuf, sem = refs[n_out:]
    i = pl.program_id(0)
    n = pl.num_programs(0)

    def start(a_ref, b_ref_, slot):
        def issue(r, c):
            _row_copy(y_hbm, a_ref[0, 0, r], ybuf.at[slot, 0], r, sem.at[slot]).start()
            _row_copy(y_hbm, b_ref_[0, 0, r], ybuf.at[slot, 1], r, sem.at[slot]).start()
            return c

        lax.fori_loop(0, rows, issue, 0, unroll=8)

    @pl.when(i == 0)
    def _():
        start(p1_ref, p2_ref, 0)

    @pl.when(i + 1 < n)
    def _():
        start(np1_ref, np2_ref, (i + 1) % 2)

    slot = i % 2
    _tile_wait(y_hbm, ybuf.at[slot, 0], sem.at[slot])
    _tile_wait(y_hbm, ybuf.at[slot, 1], sem.at[slot])
    w12 = w12_ref[...]
    ffn = w12[:, 0:1] * ybuf[slot, 0] + w12[:, 1:2] * ybuf[slot, 1]
    res = _layer_norm_rows(alpha * x_ref[...] + ffn, g_ref[...], b_ref[...])
    if n_out == 1:
        out_refs[0][...] = res
    else:
        for (lo, hi), o_ref in zip(bounds, out_refs):
            @pl.when((i >= lo) & (i < hi))
            def _(o_ref=o_ref):
                o_ref[...] = res


def _combine(x, y_sorted, pos1, pos2, w12, g, b, *, alpha, splits):
    M, D = x.shape
    rows = _pick(math.gcd(*splits), SUB_ROWS, 16)
    nt = M // rows
    p1 = pos1.reshape(nt, 1, rows)
    p2 = pos2.reshape(nt, 1, rows)
    idx = pl.BlockSpec((1, 1, rows), lambda i: (i, 0, 0), memory_space=pltpu.SMEM)
    nidx = pl.BlockSpec((1, 1, rows), lambda i: (jnp.minimum(i + 1, nt - 1), 0, 0), memory_space=pltpu.SMEM)
    row = pl.BlockSpec((rows, D), lambda i: (i, 0))
    vec = pl.BlockSpec((1, D), lambda i: (0, 0))
    bounds, out_specs, out_shape = [], [], []
    lo = 0
    for s in splits:
        nb = s // rows
        bounds.append((lo, lo + nb))
        out_specs.append(pl.BlockSpec((rows, D), lambda i, lo=lo, nb=nb: (jnp.clip(i - lo, 0, nb - 1), 0)))
        out_shape.append(jax.ShapeDtypeStruct((s, D), F32))
        lo += nb
    return pl.pallas_call(
        functools.partial(_combine_kernel, rows=rows, alpha=alpha, bounds=tuple(bounds)),
        grid=(nt,),
        in_specs=[idx, idx, nidx, nidx, row, pl.BlockSpec((rows, LANES), lambda i: (i, 0)), vec, vec,
                  pl.BlockSpec(memory_space=pl.ANY)],
        out_specs=out_specs,
        out_shape=out_shape,
        scratch_shapes=[pltpu.VMEM((2, 2, rows, D), F32), pltpu.SemaphoreType.DMA((2,))],
        compiler_params=_cp(("arbitrary",)),
        name="moe_combine",
    )(p1, p2, p1, p2, x, w12, g.reshape(1, D), b.reshape(1, D), y_sorted)


def _piece_loop(nsub, big, fn):
    nbig = lax.div(nsub, jnp.int32(big))

    def big_body(p, c):
        fn(pl.multiple_of(p * (big * SUB_ROWS), big * SUB_ROWS), big * SUB_ROWS)
        return c

    lax.fori_loop(0, nbig, big_body, 0)

    def small_body(s, c):
        fn(pl.multiple_of(s * SUB_ROWS, SUB_ROWS), SUB_ROWS)
        return c

    lax.fori_loop(nbig * big, nsub, small_body, 0)


def _zero_tail(ref, nsub, n_sub):
    def fill(s, c):
        r0 = pl.multiple_of(s * SUB_ROWS, SUB_ROWS)
        ref[pl.ds(r0, SUB_ROWS), :] = jnp.zeros((SUB_ROWS, ref.shape[1]), ref.dtype)
        return c

    lax.fori_loop(nsub, n_sub, fill, 0)


def _moe_up_kernel(exp_ref, nsub_ref, x_ref, wg_ref, wu_ref, h_ref, *, n_sub, big):
    nsub = nsub_ref[pl.program_id(0)]

    def piece(r0, rows):
        x = x_ref[pl.ds(r0, rows), :]
        g = _dot(x, wg_ref[...].astype(BF16))
        u = _dot(x, wu_ref[...].astype(BF16))
        h_ref[pl.ds(r0, rows), :] = (g * jax.nn.sigmoid(g) * u).astype(BF16)

    _piece_loop(nsub, big, piece)
    _zero_tail(h_ref, nsub, n_sub)


def _moe_up(xs, wg, wu, n_used, ch_exp, ch_nsub, *, tm_chunk, tf):
    n_slot, D = xs.shape
    F = wg.shape[2]
    tf = _pick(F, tf, LANES)
    n_sub = tm_chunk // SUB_ROWS
    return pl.pallas_call(
        functools.partial(_moe_up_kernel, n_sub=n_sub, big=min(BIG_SUBS, n_sub)),
        grid_spec=pltpu.PrefetchScalarGridSpec(
            num_scalar_prefetch=2,
            grid=(n_used, F // tf),
            in_specs=[pl.BlockSpec((tm_chunk, D), lambda c, f, e, ns: (c, 0), pipeline_mode=pl.Buffered(1)),
                      pl.BlockSpec((None, D, tf), lambda c, f, e, ns: (e[c], 0, f)),
                      pl.BlockSpec((None, D, tf), lambda c, f, e, ns: (e[c], 0, f))],
            out_specs=pl.BlockSpec((tm_chunk, tf), lambda c, f, e, ns: (c, f)),
        ),
        out_shape=jax.ShapeDtypeStruct((n_slot, F), BF16),
        compiler_params=_cp(("arbitrary", "arbitrary")),
        name="moe_up",
    )(ch_exp, ch_nsub, xs, wg, wu)


def _moe_down_kernel(exp_ref, nsub_ref, h_ref, wd_ref, y_ref, *, n_sub, big):
    k = pl.program_id(2)
    nsub = nsub_ref[pl.program_id(0)]

    @pl.when(k == 0)
    def _():
        def first(r0, rows):
            y_ref[pl.ds(r0, rows), :] = _dot(h_ref[pl.ds(r0, rows), :], wd_ref[...].astype(BF16))

        _piece_loop(nsub, big, first)
        _zero_tail(y_ref, nsub, n_sub)

    @pl.when(k > 0)
    def _():
        def acc(r0, rows):
            y_ref[pl.ds(r0, rows), :] += _dot(h_ref[pl.ds(r0, rows), :], wd_ref[...].astype(BF16))

        _piece_loop(nsub, big, acc)


def _moe_down(hs, wd, n_used, ch_exp, ch_nsub, *, tm_chunk, tn, tk):
    n_slot, F = hs.shape
    D = wd.shape[2]
    tn = _pick(D, tn, LANES)
    tk = _pick(F, tk, LANES)
    n_sub = tm_chunk // SUB_ROWS
    return pl.pallas_call(
        functools.partial(_moe_down_kernel, n_sub=n_sub, big=min(BIG_SUBS, n_sub)),
        grid_spec=pltpu.PrefetchScalarGridSpec(
            num_scalar_prefetch=2,
            grid=(n_used, D // tn, F // tk),
            in_specs=[pl.BlockSpec((tm_chunk, tk), lambda c, n, k, e, ns: (c, k)),
                      pl.BlockSpec((None, tk, tn), lambda c, n, k, e, ns: (e[c], k, n))],
            out_specs=pl.BlockSpec((tm_chunk, tn), lambda c, n, k, e, ns: (c, n)),
        ),
        out_shape=jax.ShapeDtypeStruct((n_slot, D), F32),
        compiler_params=_cp(("arbitrary", "arbitrary", "arbitrary")),
        name="moe_down",
    )(ch_exp, ch_nsub, hs, wd)


def _chunk_tables(cnt, *, n_exp, tm_chunk, n_chunk):
    cnt = cnt.astype(jnp.int32)
    nch_e = (cnt + tm_chunk - 1) // tm_chunk
    cend = jnp.cumsum(nch_e)
    cstart = cend - nch_e
    total = cend[-1]
    j = jnp.arange(n_chunk, dtype=jnp.int32)
    jj = jnp.minimum(j, total - 1)
    e = jnp.minimum(jnp.sum((jj[:, None] >= cend[None, :]).astype(jnp.int32), axis=1), n_exp - 1)
    local = jj - cstart[e]
    rows = jnp.clip(cnt[e] - local * tm_chunk, 0, tm_chunk)
    nsub = jnp.where(j < total, (rows + SUB_ROWS - 1) // SUB_ROWS, 0).astype(jnp.int32)
    base = (cstart * tm_chunk).astype(F32)
    return total.astype(jnp.int32), e.astype(jnp.int32), nsub, base


def _moe_ffn(x_f32, w_router, wg, wu, wd, ln_g, ln_b, *, alpha, splits):
    M, D = x_f32.shape
    n_exp = wg.shape[0]
    avg = TOP_K * M / n_exp
    tm_chunk = SUB_ROWS * max(1, -(-int(avg * 1.11) // SUB_ROWS))
    n_chunk = n_exp + (TOP_K * M) // tm_chunk
    n_slot = n_chunk * tm_chunk

    wr_pad = jnp.pad(w_router, ((0, 0), (0, LANES - n_exp)))
    sel, e12, w12 = _router(x_f32, wr_pad, n_exp=n_exp)
    rank, cnt = _rank(sel)
    n_used, ch_exp, ch_nsub, base = _chunk_tables(cnt[0, :n_exp], n_exp=n_exp, tm_chunk=tm_chunk,
                                                  n_chunk=n_chunk)
    base_row = jnp.pad(base, (0, LANES - n_exp)).reshape(1, LANES)
    pos = _slots(rank, e12, base_row)
    pos1, pos2 = pos[:, 0], pos[:, 1]
    src = _invert(pos1, pos2, n_slot)
    sub_per_chunk = tm_chunk // SUB_ROWS
    tile_valid = (jnp.arange(sub_per_chunk, dtype=jnp.int32)[None, :] < ch_nsub[:, None])
    tile_valid = tile_valid.astype(jnp.int32).reshape(-1)
    xs = _dispatch(x_f32, src, tile_valid)
    hs = _moe_up(xs, wg, wu, n_used, ch_exp, ch_nsub, tm_chunk=tm_chunk, tf=256)
    ys = _moe_down(hs, wd, n_used, ch_exp, ch_nsub, tm_chunk=tm_chunk, tn=1024, tk=1024)
    return _combine(x_f32, ys, pos1, pos2, w12, ln_g, ln_b, alpha=alpha, splits=splits)


def _last_rows(x2d, batch, seq, n, col0, col1):
    return jnp.stack([lax.slice(x2d, (b * seq + seq - n, col0), ((b + 1) * seq, col1)) for b in range(batch)])


def kernel(x_prompt, x_sample, state_rglru_h, state_rglru_conv, state_pool, rg_w_in, rg_conv_w, rg_conv_b,
           rg_w_a, rg_b_a, rg_w_x, rg_b_x, rg_lambda, rg_w_out, pool_w, pool_b, pool_scale, ffn_w_gate,
           ffn_w_up, ffn_w_down, moe_router, moe_w_gate, moe_w_up, moe_w_down, ln_mix_g, ln_mix_b,
           ln_ffn_g, ln_ffn_b):
    B, S, D = x_prompt.shape
    Bs, Ss, _ = x_sample.shape
    depth = ln_mix_g.shape[0]
    alpha = float((2 * depth) ** 0.25)
    d_rnn = rg_w_a.shape[1] * rg_w_a.shape[2]
    gb = rg_w_a.shape[2]
    n_pool_groups = pool_w.shape[1]
    windows = tuple(2 ** (i + 1) for i in range(n_pool_groups))
    pool_buf = state_pool.shape[2]
    conv_w = rg_conv_w.shape[1]
    Mp, Ms = B * S, Bs * Ss

    xp2 = x_prompt.reshape(Mp, D)
    xs2 = jnp.swapaxes(x_sample, 0, 1).reshape(Ms, D)
    x_parts = [xp2, xs2]
    xb = _cast_rows(x_parts)

    new_h_p, new_conv_p, new_pool_p = [], [], []
    new_h_s, new_conv_s, new_pool_s = [], [], []
    y_parts = None
    for i in range(depth):
        j = i // 2
        last = i == depth - 1
        if i % 2 == 0:
            u = _mm_fullk([xb], [rg_w_in[j]], kind="gelu_split", out_dtype=F32, tm=1024, tn=512,
                          n_gelu_cols=d_rnn, name="rg_in_proj")
            wax = jnp.concatenate([rg_w_a[j], rg_w_x[j]], axis=-1).astype(BF16)
            vec = lambda v: v.reshape(1, d_rnn)
            args = (rg_conv_w[j], vec(rg_conv_b[j]), wax, vec(rg_b_a[j]), vec(rg_b_x[j]), vec(rg_lambda[j]))
            yg_p, h_p = _rglru_prompt(u, *args, batch=B, seq=S, d_rnn=d_rnn, gb=gb)
            u_s = lax.slice(u, (Mp, 0), (Mp + Ms, 2 * d_rnn)).reshape(Ss, Bs, 2 * d_rnn)
            cs3 = jnp.swapaxes(state_rglru_conv[j], 0, 1)
            rnn_s = u_s[:, :, d_rnn:]
            yg_s, h_s = _rglru_sample(u_s[:, :, :d_rnn], rnn_s, cs3, state_rglru_h[j], *args, gb=gb)
            new_h_p.append(h_p)
            new_h_s.append(h_s)
            new_conv_p.append(_last_rows(u, B, S, conv_w - 1, d_rnn, 2 * d_rnn))
            new_conv_s.append(jnp.swapaxes(jnp.concatenate([cs3, rnn_s], axis=0)[-(conv_w - 1):], 0, 1))
            mix = _mm_fullk([yg_p, yg_s.reshape(Ms, d_rnn)], [rg_w_out[j]], kind="plain", out_dtype=F32,
                            tm=1024, tn=512, name="rg_out_proj")
            x, xb = _deepnorm(x_parts, mix, ln_mix_g[i], ln_mix_b[i], alpha=alpha, want_bf16=True, name="ln_mix")
            hid = _mm_fullk([xb], [ffn_w_gate[j], ffn_w_up[j]], kind="swiglu", out_dtype=BF16, tm=512, tn=512,
                            name="ffn_up")
            ffn = _mm_ktiled(hid, ffn_w_down[j], tm=2304, tn=1024, tk=1024, name="ffn_down")
            x, xb = _deepnorm([x], ffn, ln_ffn_g[i], ln_ffn_b[i], alpha=alpha, want_bf16=not last, name="ln_ffn")
            x_parts = [x]
        else:
            if len(x_parts) != 1:
                x = jnp.concatenate(x_parts, axis=0)
            pw = pool_w[j].astype(BF16)
            vec = lambda v: v.reshape(1, D)
            pargs = (pw, vec(pool_b[j]), vec(pool_scale[j]), vec(ln_mix_g[i]), vec(ln_mix_b[i]))
            x_s3 = lax.slice(x, (Mp, 0), (Mp + Ms, D)).reshape(Ss, Bs, D)
            st3 = jnp.swapaxes(state_pool[j], 0, 1)
            xm = _pool_prompt(x, *pargs, batch=B, seq=S, windows=windows, alpha=alpha)
            xs_f = _pool_sample(st3, x_s3, *pargs, windows=windows, pos0=PAST_LEN, alpha=alpha)
            new_pool_p.append(_last_rows(x, B, S, pool_buf, 0, D))
            new_pool_s.append(jnp.swapaxes(jnp.concatenate([st3, x_s3], axis=0)[-pool_buf:], 0, 1))
            xm = lax.dynamic_update_slice(xm, xs_f.reshape(Ms, D), (Mp, 0))
            outs = _moe_ffn(xm, moe_router[j], moe_w_gate[j], moe_w_up[j], moe_w_down[j],
                            ln_ffn_g[i], ln_ffn_b[i], alpha=alpha, splits=(Mp, Ms))
            x_parts = list(outs)
            if not last:
                xb = _cast_rows(x_parts)

    if len(x_parts) == 1:
        x_parts = [x_parts[0][:Mp], x_parts[0][Mp:]]
    y_prompt = x_parts[0].reshape(B, S, D)
    y_sample = jnp.swapaxes(x_parts[1].reshape(Ss, Bs, D), 0, 1)
    return (y_prompt, y_sample, jnp.stack(new_h_p), jnp.stack(new_conv_p), jnp.stack(new_pool_p),
            jnp.stack(new_h_s), jnp.stack(new_conv_s), jnp.stack(new_pool_s))
```

```python
import functools
import math

import jax
import jax.numpy as jnp
from jax import lax
from jax.experimental import pallas as pl
from jax.experimental.pallas import tpu as pltpu

F32 = jnp.float32
BF16 = jnp.bfloat16

LRU_C = 8.0
LN_EPS = 1e-5
PAST_LEN = 16384
TOP_K = 2
GELU_C = 0.7978845608028654

LANES = 128
SUBLANES = 8
SUB_ROWS = 256
BIG_SUBS = 4
VMEM_LIMIT = 56 * 1024 * 1024


def _pick(n, target, mult):
    best = None
    d = mult
    while d <= min(n, target):
        if n % d == 0:
            best = d
        d += mult
    return best if best is not None else n


def _cp(sem):
    return pltpu.CompilerParams(dimension_semantics=sem, vmem_limit_bytes=VMEM_LIMIT)


def _dot(a, b):
    return jnp.dot(a, b, preferred_element_type=F32)


def _gelu_tanh(x):
    return 0.5 * x * (1.0 + jnp.tanh(GELU_C * (x + 0.044715 * (x * x * x))))


def _softplus(z):
    return jnp.log1p(jnp.exp(-jnp.abs(z))) + jnp.maximum(z, 0.0)


def _sigmoid_tanh(x):
    return 0.5 + 0.5 * jnp.tanh(0.5 * x)


def _layer_norm_rows(z, g, b):
    mu = jnp.mean(z, axis=-1, keepdims=True)
    zc = z - mu
    var = jnp.mean(zc * zc, axis=-1, keepdims=True)
    return zc * lax.rsqrt(var + LN_EPS) * g + b


def _row_sources(arrs, tm, row_axis):
    specs, ranges = [], []
    lo = 0
    for a in arrs:
        nb = a.shape[0] // tm
        if nb * tm != a.shape[0]:
            raise ValueError("row source not divisible by the row tile")

        def imap(*idx, lo=lo, nb=nb):
            return (jnp.clip(idx[row_axis] - lo, 0, nb - 1), 0)

        specs.append(pl.BlockSpec((tm, a.shape[1]), imap))
        ranges.append((lo, lo + nb))
        lo += nb
    return specs, ranges, lo


def _for_row_source(m, ranges, refs, body):
    if len(refs) == 1:
        body(refs[0])
        return
    for (lo, hi), r in zip(ranges, refs):
        @pl.when((m >= lo) & (m < hi))
        def _(r=r):
            body(r)


def _common_tile(arrs, target, mult):
    return _pick(math.gcd(*[a.shape[0] for a in arrs]), target, mult)


def _cast_kernel(*refs, n_x, ranges):
    o_ref = refs[n_x]

    def body(x_ref):
        o_ref[...] = x_ref[...].astype(BF16)

    _for_row_source(pl.program_id(0), ranges, refs[:n_x], body)


def _cast_rows(xs):
    D = xs[0].shape[1]
    tm = _common_tile(xs, 512, 16)
    x_specs, ranges, n_m = _row_sources(xs, tm, row_axis=0)
    return pl.pallas_call(
        functools.partial(_cast_kernel, n_x=len(xs), ranges=ranges),
        grid=(n_m,),
        in_specs=x_specs,
        out_specs=pl.BlockSpec((tm, D), lambda i: (i, 0)),
        out_shape=jax.ShapeDtypeStruct((n_m * tm, D), BF16),
        compiler_params=_cp(("arbitrary",)),
        name="cast_rows",
    )(*xs)


def _mm_fullk_kernel(*refs, kind, n_x, n_w, ranges, n_gelu_tiles):
    x_refs = refs[:n_x]
    w_refs = refs[n_x:n_x + n_w]
    out_ref = refs[n_x + n_w]
    n = pl.program_id(0)
    m = pl.program_id(1)

    def body(x_ref):
        x = x_ref[...]
        if kind == "swiglu":
            g = _dot(x, w_refs[0][...].astype(BF16))
            u = _dot(x, w_refs[1][...].astype(BF16))
            out_ref[...] = (g * jax.nn.sigmoid(g) * u).astype(out_ref.dtype)
        elif kind == "gelu_split":
            u = _dot(x, w_refs[0][...].astype(BF16))

            @pl.when(n < n_gelu_tiles)
            def _():
                out_ref[...] = _gelu_tanh(u)

            @pl.when(n >= n_gelu_tiles)
            def _():
                out_ref[...] = u
        else:
            out_ref[...] = _dot(x, w_refs[0][...].astype(BF16)).astype(out_ref.dtype)

    _for_row_source(m, ranges, x_refs, body)


def _mm_fullk(xs, ws, *, kind, out_dtype, tm, tn, n_gelu_cols=0, name):
    K = xs[0].shape[1]
    N = ws[0].shape[1]
    tm = _common_tile(xs, tm, 16)
    tn = _pick(N, tn, LANES)
    x_specs, ranges, n_m = _row_sources(xs, tm, row_axis=1)
    kern = functools.partial(_mm_fullk_kernel, kind=kind, n_x=len(xs), n_w=len(ws), ranges=ranges,
                             n_gelu_tiles=n_gelu_cols // tn)
    return pl.pallas_call(
        kern,
        grid=(N // tn, n_m),
        in_specs=x_specs + [pl.BlockSpec((K, tn), lambda n, m: (0, n)) for _ in ws],
        out_specs=pl.BlockSpec((tm, tn), lambda n, m: (m, n)),
        out_shape=jax.ShapeDtypeStruct((n_m * tm, N), out_dtype),
        compiler_params=_cp(("arbitrary", "arbitrary")),
        name=name,
    )(*xs, *ws)


def _mm_ktiled_kernel(x_ref, w_ref, o_ref, *, n_piece):
    k = pl.program_id(2)
    rows = x_ref.shape[0] // n_piece

    @pl.when(k == 0)
    def _():
        for p in range(n_piece):
            o_ref[p * rows:(p + 1) * rows, :] = _dot(x_ref[p * rows:(p + 1) * rows, :], w_ref[...].astype(BF16))

    @pl.when(k > 0)
    def _():
        for p in range(n_piece):
            o_ref[p * rows:(p + 1) * rows, :] += _dot(x_ref[p * rows:(p + 1) * rows, :], w_ref[...].astype(BF16))


def _mm_ktiled(x, w, *, tm, tn, tk, name):
    M, K = x.shape
    N = w.shape[1]
    tm = _pick(M, tm, 16)
    tn = _pick(N, tn, LANES)
    tk = _pick(K, tk, LANES)
    n_piece = 3 if tm % (3 * 16) == 0 else 1
    return pl.pallas_call(
        functools.partial(_mm_ktiled_kernel, n_piece=n_piece),
        grid=(M // tm, N // tn, K // tk),
        in_specs=[pl.BlockSpec((tm, tk), lambda m, n, k: (m, k)),
                  pl.BlockSpec((tk, tn), lambda m, n, k: (k, n))],
        out_specs=pl.BlockSpec((tm, tn), lambda m, n, k: (m, n)),
        out_shape=jax.ShapeDtypeStruct((M, N), F32),
        compiler_params=_cp(("arbitrary", "arbitrary", "arbitrary")),
        name=name,
    )(x, w)


def _ln_kernel(*refs, alpha, n_x, ranges, want_bf16):
    x_refs = refs[:n_x]
    y_ref, g_ref, b_ref, of_ref = refs[n_x:n_x + 4]
    i = pl.program_id(0)

    def body(x_ref):
        o = _layer_norm_rows(alpha * x_ref[...] + y_ref[...], g_ref[...], b_ref[...])
        of_ref[...] = o
        if want_bf16:
            refs[n_x + 4][...] = o.astype(BF16)

    _for_row_source(i, ranges, x_refs, body)


def _deepnorm(xs, y, g, b, *, alpha, want_bf16, name):
    M, D = y.shape
    tm = _common_tile(xs, 256, 16)
    x_specs, ranges, n_m = _row_sources(xs, tm, row_axis=0)
    row = pl.BlockSpec((tm, D), lambda i: (i, 0))
    vec = pl.BlockSpec((1, D), lambda i: (0, 0))
    out_specs = [row, row] if want_bf16 else [row]
    out_shape = [jax.ShapeDtypeStruct((M, D), F32)] + ([jax.ShapeDtypeStruct((M, D), BF16)] if want_bf16 else [])
    outs = pl.pallas_call(
        functools.partial(_ln_kernel, alpha=alpha, n_x=len(xs), ranges=ranges, want_bf16=want_bf16),
        grid=(n_m,),
        in_specs=x_specs + [row, vec, vec],
        out_specs=out_specs,
        out_shape=out_shape,
        compiler_params=_cp(("arbitrary",)),
        name=name,
    )(*xs, y, g.reshape(1, D), b.reshape(1, D))
    return (outs[0], outs[1]) if want_bf16 else (outs[0], None)


def _rglru_coeffs(xc, wax_ref, ba, bx, lam, gb):
    xcb = xc.astype(BF16)
    nblk = xc.shape[1] // gb
    r_parts, i_parts = [], []
    for j in range(nblk):
        ru = _dot(xcb[:, j * gb:(j + 1) * gb], wax_ref[j])
        r_parts.append(ru[:, :gb])
        i_parts.append(ru[:, gb:])
    r = _sigmoid_tanh(jnp.concatenate(r_parts, axis=1) + ba)
    ig = _sigmoid_tanh(jnp.concatenate(i_parts, axis=1) + bx)
    log_a = (-LRU_C * r) * _softplus(-lam)
    t = jnp.tanh(0.5 * log_a)
    q = 1.0 / (1.0 - t)
    a = (1.0 + t) * q
    one_minus_a2 = (-4.0 * t) * (q * q)
    b = jnp.sqrt(one_minus_a2) * (ig * xc)
    return a, b


def _rglru_prompt_kernel(gate_ref, rnn_ref, cw_ref, cb_ref, wax_ref, ba_ref, bx_ref, lam_ref,
                         yg_ref, hlast_ref, xp_scr, h_scr, a_scr, b_scr, *, tc, gb, cw):
    b_idx = pl.program_id(1)
    i = pl.program_id(2)
    nt = pl.num_programs(2)
    halo = SUBLANES

    @pl.when(i == 0)
    def _():
        xp_scr[0:halo, :] = jnp.zeros((halo, xp_scr.shape[1]), F32)
        h_scr[...] = jnp.zeros(h_scr.shape, F32)

    xp_scr[halo:halo + tc, :] = rnn_ref[...]
    xc = cb_ref[...]
    for k in range(cw):
        xc = xc + xp_scr[pl.ds(halo - (cw - 1) + k, tc), :] * cw_ref[k:k + 1, :]
    a, b = _rglru_coeffs(xc, wax_ref, ba_ref[...], bx_ref[...], lam_ref[...], gb)
    a_scr[...] = a
    b_scr[...] = b

    def step(t, h):
        h = a_scr[pl.ds(t, 1), :] * h + b_scr[pl.ds(t, 1), :]
        b_scr[pl.ds(t, 1), :] = h
        return h

    h = lax.fori_loop(0, tc, step, h_scr[0:1, :], unroll=8)
    h_scr[0:1, :] = h
    yg_ref[...] = (b_scr[...] * gate_ref[...]).astype(BF16)
    xp_scr[0:halo, :] = xp_scr[tc:tc + halo, :]

    @pl.when(i == nt - 1)
    def _():
        hlast_ref[pl.ds(b_idx, 1), :] = h


def _rglru_prompt(u, cwt, cbias, wax, ba, bx, lam, *, batch, seq, d_rnn, gb):
    C = _pick(d_rnn, 1024, gb)
    tc = _pick(seq, 256, SUBLANES)
    nt = seq // tc
    nc = d_rnn // C
    cw = cwt.shape[0]
    vec = pl.BlockSpec((1, C), lambda c, b, i: (0, c))
    kern = functools.partial(_rglru_prompt_kernel, tc=tc, gb=gb, cw=cw)
    return pl.pallas_call(
        kern,
        grid=(nc, batch, nt),
        in_specs=[pl.BlockSpec((tc, C), lambda c, b, i: (b * nt + i, c)),
                  pl.BlockSpec((tc, C), lambda c, b, i: (b * nt + i, nc + c)),
                  pl.BlockSpec((cw, C), lambda c, b, i: (0, c)),
                  vec,
                  pl.BlockSpec((C // gb, gb, 2 * gb), lambda c, b, i: (c, 0, 0)),
                  vec, vec, vec],
        out_specs=[pl.BlockSpec((tc, C), lambda c, b, i: (b * nt + i, c)),
                   pl.BlockSpec((batch, C), lambda c, b, i: (0, c))],
        out_shape=[jax.ShapeDtypeStruct((batch * seq, d_rnn), BF16),
                   jax.ShapeDtypeStruct((batch, d_rnn), F32)],
        scratch_shapes=[pltpu.VMEM((tc + SUBLANES, C), F32), pltpu.VMEM((SUBLANES, C), F32),
                        pltpu.VMEM((tc, C), F32), pltpu.VMEM((tc, C), F32)],
        compiler_params=_cp(("arbitrary", "arbitrary", "arbitrary")),
        name="rglru_prompt",
    )(u, u, cwt, cbias, wax, ba, bx, lam)


def _rglru_sample_kernel(gate_ref, rnn_ref, cs_ref, h0_ref, cw_ref, cb_ref, wax_ref, ba_ref, bx_ref,
                         lam_ref, yg_ref, hlast_ref, xc_scr, *, steps, bsz, gb, cw):
    slabs = [cs_ref[k] for k in range(cw - 1)] + [rnn_ref[t * bsz:(t + 1) * bsz, :] for t in range(steps)]
    for t in range(steps):
        xc = cb_ref[...]
        for k in range(cw):
            xc = xc + slabs[t + k] * cw_ref[k:k + 1, :]
        xc_scr[t * bsz:(t + 1) * bsz, :] = xc
    a, b = _rglru_coeffs(xc_scr[...], wax_ref, ba_ref[...], bx_ref[...], lam_ref[...], gb)
    h = h0_ref[...]
    for t in range(steps):
        h = a[t * bsz:(t + 1) * bsz, :] * h + b[t * bsz:(t + 1) * bsz, :]
        yg_ref[t] = (h * gate_ref[t * bsz:(t + 1) * bsz, :]).astype(BF16)
    hlast_ref[...] = h


def _rglru_sample(u, row0, steps, cs3, h0, cwt, cbias, wax, ba, bx, lam, *, gb):
    bsz, d_rnn = h0.shape
    rows = steps * bsz
    if row0 % rows:
        u = lax.slice(u, (row0, 0), (row0 + rows, u.shape[1]))
        row0 = 0
    rblk = row0 // rows
    C = _pick(d_rnn, 1024, gb)
    nc = d_rnn // C
    cw = cwt.shape[0]
    vec = pl.BlockSpec((1, C), lambda c: (0, c))
    kern = functools.partial(_rglru_sample_kernel, steps=steps, bsz=bsz, gb=gb, cw=cw)
    return pl.pallas_call(
        kern,
        grid=(nc,),
        in_specs=[pl.BlockSpec((rows, C), lambda c: (rblk, c)),
                  pl.BlockSpec((rows, C), lambda c: (rblk, nc + c)),
                  pl.BlockSpec((cw - 1, bsz, C), lambda c: (0, 0, c)),
                  pl.BlockSpec((bsz, C), lambda c: (0, c)),
                  pl.BlockSpec((cw, C), lambda c: (0, c)),
                  vec,
                  pl.BlockSpec((C // gb, gb, 2 * gb), lambda c: (c, 0, 0)),
                  vec, vec, vec],
        out_specs=[pl.BlockSpec((steps, bsz, C), lambda c: (0, 0, c)),
                   pl.BlockSpec((bsz, C), lambda c: (0, c))],
        out_shape=[jax.ShapeDtypeStruct((steps, bsz, d_rnn), BF16),
                   jax.ShapeDtypeStruct((bsz, d_rnn), F32)],
        scratch_shapes=[pltpu.VMEM((steps * bsz, C), F32)],
        compiler_params=_cp(("arbitrary",)),
        name="rglru_sample",
    )(u, u, cs3, h0, cwt, cbias, wax, ba, bx, lam)


def _pool_prompt_kernel(x_ref, pw_ref, pb_ref, ps_ref, g_ref, b_ref, of_ref, xp_scr, z_scr,
                        *, tm, batch, windows, alpha):
    bb = pl.program_id(0)
    i = pl.program_id(1)
    halo = 2 * SUBLANES
    D = x_ref.shape[1]
    pg = D // len(windows)

    @pl.when(bb < batch)
    def _():
        @pl.when(i == 0)
        def _():
            xp_scr[0:halo, :] = jnp.zeros((halo, D), F32)

        xp_scr[halo:halo + tm, :] = x_ref[...]
        pos = i * tm + lax.broadcasted_iota(jnp.int32, (tm, 1), 0)
        for gi, w in enumerate(windows):
            cols = slice(gi * pg, (gi + 1) * pg)
            xg = x_ref[:, cols]
            s = xg
            for j in range(1, w):
                s = s + xp_scr[pl.ds(halo - j, tm), cols]
            cnt = jnp.minimum(w, pos + 1).astype(F32)
            mixed = (s / cnt - xg).astype(BF16)
            out = (_dot(mixed, pw_ref[gi]) + pb_ref[:, cols]) * ps_ref[:, cols]
            z_scr[:, cols] = alpha * xg + out
        of_ref[...] = _layer_norm_rows(z_scr[...], g_ref[...], b_ref[...])
        xp_scr[0:halo, :] = xp_scr[tm:tm + halo, :]

    @pl.when(bb >= batch)
    def _():
        of_ref[...] = jnp.zeros(of_ref.shape, F32)


def _pool_prompt(x, pw, pb, ps, g, b, *, batch, seq, windows, alpha):
    M, D = x.shape
    tm = _pick(math.gcd(seq, M - batch * seq) if M > batch * seq else seq, 256, 16)
    nt = seq // tm
    n_extra = (M - batch * seq) // tm
    if n_extra > nt:
        raise ValueError("extra rows must fit in one pass of the time-tile axis")
    ng = len(windows)

    def rmap(bb, i):
        blk = jnp.where(bb < batch, bb * nt + i, batch * nt + jnp.minimum(i, max(n_extra - 1, 0)))
        return (blk, 0)

    row = pl.BlockSpec((tm, D), rmap)
    vec = pl.BlockSpec((1, D), lambda bb, i: (0, 0))
    kern = functools.partial(_pool_prompt_kernel, tm=tm, batch=batch, windows=windows, alpha=alpha)
    return pl.pallas_call(
        kern,
        grid=(batch + (1 if n_extra else 0), nt),
        in_specs=[row, pl.BlockSpec((ng, D // ng, D // ng), lambda bb, i: (0, 0, 0)), vec, vec, vec, vec],
        out_specs=row,
        out_shape=jax.ShapeDtypeStruct((M, D), F32),
        scratch_shapes=[pltpu.VMEM((tm + 2 * SUBLANES, D), F32), pltpu.VMEM((tm, D), F32)],
        compiler_params=_cp(("arbitrary", "arbitrary")),
        name="pool_prompt",
    )(x, pw, pb, ps, g, b)


def _pool_sample_kernel(st_ref, x_ref, pw_ref, pb_ref, ps_ref, g_ref, b_ref, of_ref,
                        mix_scr, z_scr, *, steps, bt, windows, pos0, alpha):
    nbuf = st_ref.shape[0]
    D = x_ref.shape[2]
    pg = D // len(windows)

    def slab(p, cols):
        return st_ref[p, :, cols] if p < nbuf else x_ref[p - nbuf, :, cols]

    for gi, w in enumerate(windows):
        cols = slice(gi * pg, (gi + 1) * pg)
        for t in range(steps):
            xg = x_ref[t, :, cols]
            s = xg
            for j in range(1, w):
                s = s + slab(nbuf + t - j, cols)
            cnt = float(min(w, pos0 + t + 1))
            mix_scr[t * bt:(t + 1) * bt, cols] = (s / cnt - xg).astype(BF16)
            z_scr[t * bt:(t + 1) * bt, cols] = alpha * xg
    for gi in range(len(windows)):
        cols = slice(gi * pg, (gi + 1) * pg)
        out = (_dot(mix_scr[:, cols], pw_ref[gi]) + pb_ref[:, cols]) * ps_ref[:, cols]
        z_scr[:, cols] = z_scr[:, cols] + out
    o = _layer_norm_rows(z_scr[...], g_ref[...], b_ref[...])
    for t in range(steps):
        of_ref[t] = o[t * bt:(t + 1) * bt, :]


def _pool_sample(st3, x3, pw, pb, ps, g, b, *, windows, pos0, alpha):
    nbuf, bsz, D = st3.shape
    steps = x3.shape[0]
    bt = _pick(bsz, 16, 16)
    ng = len(windows)
    vec = pl.BlockSpec((1, D), lambda j: (0, 0))
    blk = pl.BlockSpec((steps, bt, D), lambda j: (0, j, 0))
    kern = functools.partial(_pool_sample_kernel, steps=steps, bt=bt, windows=windows, pos0=pos0, alpha=alpha)
    return pl.pallas_call(
        kern,
        grid=(bsz // bt,),
        in_specs=[pl.BlockSpec((nbuf, bt, D), lambda j: (0, j, 0)), blk,
                  pl.BlockSpec((ng, D // ng, D // ng), lambda j: (0, 0, 0)), vec, vec, vec, vec],
        out_specs=blk,
        out_shape=jax.ShapeDtypeStruct((steps, bsz, D), F32),
        scratch_shapes=[pltpu.VMEM((steps * bt, D), BF16), pltpu.VMEM((steps * bt, D), F32)],
        compiler_params=_cp(("arbitrary",)),
        name="pool_sample",
    )(st3, x3, pw, pb, ps, g, b)


def _router_kernel(x_ref, wr_ref, sel_ref, e12_ref, w12_ref, *, n_exp):
    x = x_ref[...]
    xh = x.astype(BF16)
    xl = (x - xh.astype(F32)).astype(BF16)
    w = wr_ref[...]
    wh = w.astype(BF16)
    wl = (w - wh.astype(F32)).astype(BF16)
    logits = _dot(xh, wh) + (_dot(xh, wl) + _dot(xl, wh))
    lane = lax.broadcasted_iota(jnp.int32, logits.shape, 1)
    neg = jnp.float32(-jnp.inf)
    logits = jnp.where(lane < n_exp, logits, neg)
    m1 = jnp.max(logits, axis=-1, keepdims=True)
    i1 = jnp.min(jnp.where(logits == m1, lane, LANES), axis=-1, keepdims=True)
    rest = jnp.where(lane == i1, neg, logits)
    m2 = jnp.max(rest, axis=-1, keepdims=True)
    i2 = jnp.min(jnp.where(rest == m2, lane, LANES), axis=-1, keepdims=True)
    e = jnp.exp(m2 - m1)
    den = 1.0 + e
    w1 = 1.0 / den
    w2 = e / den
    sel_ref[...] = jnp.where((lane == i1) | (lane == i2), 1.0, 0.0).astype(F32)
    e12_ref[...] = jnp.where(lane == 0, i1, jnp.where(lane == 1, i2, 0))
    w12_ref[...] = jnp.where(lane == 0, w1, jnp.where(lane == 1, w2, 0.0))


def _router(x, wr_pad, *, n_exp):
    M, D = x.shape
    tm = _pick(M, 512, 16)
    row = pl.BlockSpec((tm, LANES), lambda i: (i, 0))
    return pl.pallas_call(
        functools.partial(_router_kernel, n_exp=n_exp),
        grid=(M // tm,),
        in_specs=[pl.BlockSpec((tm, D), lambda i: (i, 0)), pl.BlockSpec((D, LANES), lambda i: (0, 0))],
        out_specs=[row, row, row],
        out_shape=[jax.ShapeDtypeStruct((M, LANES), F32), jax.ShapeDtypeStruct((M, LANES), jnp.int32),
                   jax.ShapeDtypeStruct((M, LANES), F32)],
        compiler_params=_cp(("arbitrary",)),
        name="moe_router",
    )(x, wr_pad)


def _rank_kernel(sel_ref, rank_ref, cnt_ref, carry):
    i = pl.program_id(0)

    @pl.when(i == 0)
    def _():
        carry[...] = jnp.zeros(carry.shape, F32)

    sel = sel_ref[...]
    tm = sel.shape[0]
    r = lax.broadcasted_iota(jnp.int32, (tm, tm), 0)
    c = lax.broadcasted_iota(jnp.int32, (tm, tm), 1)
    tri = jnp.where(c < r, 1.0, 0.0).astype(BF16)
    excl = _dot(tri, sel.astype(BF16)) + carry[0:1, :]
    rank_ref[...] = excl
    tot = excl[tm - 1:tm, :] + sel[tm - 1:tm, :]
    carry[0:1, :] = tot
    cnt_ref[...] = jnp.broadcast_to(tot, cnt_ref.shape)


def _rank(sel):
    M = sel.shape[0]
    tm = _pick(M, 256, 16)
    return pl.pallas_call(
        _rank_kernel,
        grid=(M // tm,),
        in_specs=[pl.BlockSpec((tm, LANES), lambda i: (i, 0))],
        out_specs=[pl.BlockSpec((tm, LANES), lambda i: (i, 0)), pl.BlockSpec((SUBLANES, LANES), lambda i: (0, 0))],
        out_shape=[jax.ShapeDtypeStruct((M, LANES), F32), jax.ShapeDtypeStruct((SUBLANES, LANES), F32)],
        scratch_shapes=[pltpu.VMEM((SUBLANES, LANES), F32)],
        compiler_params=_cp(("arbitrary",)),
        name="moe_rank",
    )(sel)


def _slot_kernel(rank_ref, e12_ref, base_ref, pos_ref):
    lane = lax.broadcasted_iota(jnp.int32, rank_ref.shape, 1)
    slot = rank_ref[...] + base_ref[...]
    e12 = e12_ref[...]
    p1 = jnp.sum(jnp.where(lane == e12[:, 0:1], slot, 0.0), axis=-1, keepdims=True)
    p2 = jnp.sum(jnp.where(lane == e12[:, 1:2], slot, 0.0), axis=-1, keepdims=True)
    pos_ref[...] = jnp.where(lane == 0, p1, jnp.where(lane == 1, p2, 0.0)).astype(jnp.int32)


def _slots(rank, e12, base_row):
    M = rank.shape[0]
    tm = _pick(M, 512, 16)
    row = pl.BlockSpec((tm, LANES), lambda i: (i, 0))
    return pl.pallas_call(
        _slot_kernel,
        grid=(M // tm,),
        in_specs=[row, row, pl.BlockSpec((1, LANES), lambda i: (0, 0))],
        out_specs=row,
        out_shape=jax.ShapeDtypeStruct((M, LANES), jnp.int32),
        compiler_params=_cp(("arbitrary",)),
        name="moe_slots",
    )(rank, e12, base_row)


def _invert_kernel(p1_ref, p2_ref, src_ref, *, n_tok, n_slot):
    def zero(s, c):
        src_ref[s] = 0
        return c

    lax.fori_loop(0, n_slot, zero, 0, unroll=8)

    def put(t, c):
        src_ref[p1_ref[t]] = t
        src_ref[p2_ref[t]] = t
        return c

    lax.fori_loop(0, n_tok, put, 0, unroll=8)


def _invert(p1, p2, n_slot):
    n_tok = p1.shape[0]
    smem = pl.BlockSpec(memory_space=pltpu.SMEM)
    return pl.pallas_call(
        functools.partial(_invert_kernel, n_tok=n_tok, n_slot=n_slot),
        in_specs=[smem, smem],
        out_specs=smem,
        out_shape=jax.ShapeDtypeStruct((n_slot,), jnp.int32),
        name="moe_invert",
    )(p1, p2)


def _row_copy(src_hbm, row, dst, i, sem):
    return pltpu.make_async_copy(src_hbm.at[pl.ds(row, 1), :], dst.at[pl.ds(i, 1), :], sem)


def _tile_wait(src_hbm, dst, sem):
    pltpu.make_async_copy(src_hbm.at[pl.ds(0, dst.shape[0]), :], dst, sem).wait()


def _dispatch_kernel(valid_ref, src_ref, nsrc_ref, x_hbm, out_ref, buf, sem, *, rows):
    q = pl.program_id(0)
    nq = pl.num_programs(0)

    def start(idx_ref, slot):
        def issue(i, c):
            _row_copy(x_hbm, idx_ref[0, 0, i], buf.at[slot], i, sem.at[slot]).start()
            return c

        lax.fori_loop(0, rows, issue, 0, unroll=8)

    @pl.when((q == 0) & (valid_ref[0] > 0))
    def _():
        start(src_ref, 0)

    nxt = jnp.minimum(q + 1, nq - 1)

    @pl.when((q + 1 < nq) & (valid_ref[nxt] > 0))
    def _():
        start(nsrc_ref, (q + 1) % 2)

    slot = q % 2

    @pl.when(valid_ref[q] > 0)
    def _():
        _tile_wait(x_hbm, buf.at[slot], sem.at[slot])
        out_ref[...] = buf[slot].astype(BF16)

    @pl.when(valid_ref[q] == 0)
    def _():
        out_ref[...] = jnp.zeros(out_ref.shape, BF16)


def _dispatch(x, src, tile_valid):
    n_slot = src.shape[0]
    D = x.shape[1]
    rows = SUB_ROWS
    nt = n_slot // rows
    src3 = src.reshape(nt, 1, rows)
    return pl.pallas_call(
        functools.partial(_dispatch_kernel, rows=rows),
        grid_spec=pltpu.PrefetchScalarGridSpec(
            num_scalar_prefetch=1,
            grid=(nt,),
            in_specs=[pl.BlockSpec((1, 1, rows), lambda q, v: (q, 0, 0), memory_space=pltpu.SMEM),
                      pl.BlockSpec((1, 1, rows), lambda q, v: (jnp.minimum(q + 1, nt - 1), 0, 0),
                                   memory_space=pltpu.SMEM),
                      pl.BlockSpec(memory_space=pl.ANY)],
            out_specs=pl.BlockSpec((rows, D), lambda q, v: (q, 0)),
            scratch_shapes=[pltpu.VMEM((2, rows, D), F32), pltpu.SemaphoreType.DMA((2,))],
        ),
        out_shape=jax.ShapeDtypeStruct((n_slot, D), BF16),
        compiler_params=_cp(("arbitrary",)),
        name="moe_dispatch",
    )(tile_valid, src3, src3, x)


def _combine_kernel(p1_ref, p2_ref, np1_ref, np2_ref, x_ref, w12_ref, g_ref, b_ref, y_hbm, *refs,
                    rows, alpha, bounds):
    n_out = len(bounds)
    out_refs = refs[:n_out]
    ybuf, sem = refs[n_out:]
    i = pl.program_id(0)
    n = pl.num_programs(0)

    def start(a_ref, b_ref_, slot):
        def issue(r, c):
            _row_copy(y_hbm, a_ref[0, 0, r], ybuf.at[slot, 0], r, sem.at[slot]).start()
            _row_copy(y_hbm, b_ref_[0, 0, r], ybuf.at[slot, 1], r, sem.at[slot]).start()
            return c

        lax.fori_loop(0, rows, issue, 0, unroll=8)

    @pl.when(i == 0)
    def _():
        start(p1_ref, p2_ref, 0)

    @pl.when(i + 1 < n)
    def _():
        start(np1_ref, np2_ref, (i + 1) % 2)

    slot = i % 2
    _tile_wait(y_hbm, ybuf.at[slot, 0], sem.at[slot])
    _tile_wait(y_hbm, ybuf.at[slot, 1], sem.at[slot])
    w12 = w12_ref[...]
    ffn = w12[:, 0:1] * ybuf[slot, 0] + w12[:, 1:2] * ybuf[slot, 1]
    res = _layer_norm_rows(alpha * x_ref[...] + ffn, g_ref[...], b_ref[...])
    if n_out == 1:
        out_refs[0][...] = res
    else:
        for (lo, hi), o_ref in zip(bounds, out_refs):
            @pl.when((i >= lo) & (i < hi))
            def _(o_ref=o_ref):
                o_ref[...] = res


def _combine(x, y_sorted, pos1, pos2, w12, g, b, *, alpha, splits):
    M, D = x.shape
    rows = _pick(math.gcd(*splits), SUB_ROWS, 16)
    nt = M // rows
    p1 = pos1.reshape(nt, 1, rows)
    p2 = pos2.reshape(nt, 1, rows)
    idx = pl.BlockSpec((1, 1, rows), lambda i: (i, 0, 0), memory_space=pltpu.SMEM)
    nidx = pl.BlockSpec((1, 1, rows), lambda i: (jnp.minimum(i + 1, nt - 1), 0, 0), memory_space=pltpu.SMEM)
    row = pl.BlockSpec((rows, D), lambda i: (i, 0))
    vec = pl.BlockSpec((1, D), lambda i: (0, 0))
    bounds, out_specs, out_shape = [], [], []
    lo = 0
    for s in splits:
        nb = s // rows
        bounds.append((lo, lo + nb))
        out_specs.append(pl.BlockSpec((rows, D), lambda i, lo=lo, nb=nb: (jnp.clip(i - lo, 0, nb - 1), 0)))
        out_shape.append(jax.ShapeDtypeStruct((s, D), F32))
        lo += nb
    return pl.pallas_call(
        functools.partial(_combine_kernel, rows=rows, alpha=alpha, bounds=tuple(bounds)),
        grid=(nt,),
        in_specs=[idx, idx, nidx, nidx, row, pl.BlockSpec((rows, LANES), lambda i: (i, 0)), vec, vec,
                  pl.BlockSpec(memory_space=pl.ANY)],
        out_specs=out_specs,
        out_shape=out_shape,
        scratch_shapes=[pltpu.VMEM((2, 2, rows, D), F32), pltpu.SemaphoreType.DMA((2,))],
        compiler_params=_cp(("arbitrary",)),
        name="moe_combine",
    )(p1, p2, p1, p2, x, w12, g.reshape(1, D), b.reshape(1, D), y_sorted)


def _piece_loop(nsub, big, fn):
    nbig = lax.div(nsub, jnp.int32(big))

    def big_body(p, c):
        fn(pl.multiple_of(p * (big * SUB_ROWS), big * SUB_ROWS), big * SUB_ROWS)
        return c

    lax.fori_loop(0, nbig, big_body, 0)

    def small_body(s, c):
        fn(pl.multiple_of(s * SUB_ROWS, SUB_ROWS), SUB_ROWS)
        return c

    lax.fori_loop(nbig * big, nsub, small_body, 0)


def _zero_tail(ref, nsub, n_sub):
    def fill(s, c):
        r0 = pl.multiple_of(s * SUB_ROWS, SUB_ROWS)
        ref[pl.ds(r0, SUB_ROWS), :] = jnp.zeros((SUB_ROWS, ref.shape[1]), ref.dtype)
        return c

    lax.fori_loop(nsub, n_sub, fill, 0)


def _moe_up_kernel(exp_ref, nsub_ref, x_ref, wg_ref, wu_ref, h_ref, *, n_sub, big):
    nsub = nsub_ref[pl.program_id(0)]

    def piece(r0, rows):
        x = x_ref[pl.ds(r0, rows), :]
        g = _dot(x, wg_ref[...].astype(BF16))
        u = _dot(x, wu_ref[...].astype(BF16))
        h_ref[pl.ds(r0, rows), :] = (g * jax.nn.sigmoid(g) * u).astype(BF16)

    _piece_loop(nsub, big, piece)
    _zero_tail(h_ref, nsub, n_sub)


def _moe_up(xs, wg, wu, n_used, ch_exp, ch_nsub, *, tm_chunk, tf):
    n_slot, D = xs.shape
    F = wg.shape[2]
    tf = _pick(F, tf, LANES)
    n_sub = tm_chunk // SUB_ROWS
    return pl.pallas_call(
        functools.partial(_moe_up_kernel, n_sub=n_sub, big=min(BIG_SUBS, n_sub)),
        grid_spec=pltpu.PrefetchScalarGridSpec(
            num_scalar_prefetch=2,
            grid=(n_used, F // tf),
            in_specs=[pl.BlockSpec((tm_chunk, D), lambda c, f, e, ns: (c, 0), pipeline_mode=pl.Buffered(1)),
                      pl.BlockSpec((None, D, tf), lambda c, f, e, ns: (e[c], 0, f)),
                      pl.BlockSpec((None, D, tf), lambda c, f, e, ns: (e[c], 0, f))],
            out_specs=pl.BlockSpec((tm_chunk, tf), lambda c, f, e, ns: (c, f)),
        ),
        out_shape=jax.ShapeDtypeStruct((n_slot, F), BF16),
        compiler_params=_cp(("arbitrary", "arbitrary")),
        name="moe_up",
    )(ch_exp, ch_nsub, xs, wg, wu)


def _moe_down_kernel(exp_ref, nsub_ref, h_ref, wd_ref, y_ref, *, n_sub, big):
    k = pl.program_id(2)
    nsub = nsub_ref[pl.program_id(0)]

    @pl.when(k == 0)
    def _():
        def first(r0, rows):
            y_ref[pl.ds(r0, rows), :] = _dot(h_ref[pl.ds(r0, rows), :], wd_ref[...].astype(BF16))

        _piece_loop(nsub, big, first)
        _zero_tail(y_ref, nsub, n_sub)

    @pl.when(k > 0)
    def _():
        def acc(r0, rows):
            y_ref[pl.ds(r0, rows), :] += _dot(h_ref[pl.ds(r0, rows), :], wd_ref[...].astype(BF16))

        _piece_loop(nsub, big, acc)


def _moe_down(hs, wd, n_used, ch_exp, ch_nsub, *, tm_chunk, tn, tk):
    n_slot, F = hs.shape
    D = wd.shape[2]
    tn = _pick(D, tn, LANES)
    tk = _pick(F, tk, LANES)
    n_sub = tm_chunk // SUB_ROWS
    return pl.pallas_call(
        functools.partial(_moe_down_kernel, n_sub=n_sub, big=min(BIG_SUBS, n_sub)),
        grid_spec=pltpu.PrefetchScalarGridSpec(
            num_scalar_prefetch=2,
            grid=(n_used, D // tn, F // tk),
            in_specs=[pl.BlockSpec((tm_chunk, tk), lambda c, n, k, e, ns: (c, k)),
                      pl.BlockSpec((None, tk, tn), lambda c, n, k, e, ns: (e[c], k, n))],
            out_specs=pl.BlockSpec((tm_chunk, tn), lambda c, n, k, e, ns: (c, n)),
        ),
        out_shape=jax.ShapeDtypeStruct((n_slot, D), F32),
        compiler_params=_cp(("arbitrary", "arbitrary", "arbitrary")),
        name="moe_down",
    )(ch_exp, ch_nsub, hs, wd)


def _chunk_tables(cnt, *, n_exp, tm_chunk, n_chunk):
    cnt = cnt.astype(jnp.int32)
    nch_e = (cnt + tm_chunk - 1) // tm_chunk
    cend = jnp.cumsum(nch_e)
    cstart = cend - nch_e
    total = cend[-1]
    j = jnp.arange(n_chunk, dtype=jnp.int32)
    jj = jnp.minimum(j, total - 1)
    e = jnp.minimum(jnp.sum((jj[:, None] >= cend[None, :]).astype(jnp.int32), axis=1), n_exp - 1)
    local = jj - cstart[e]
    rows = jnp.clip(cnt[e] - local * tm_chunk, 0, tm_chunk)
    nsub = jnp.where(j < total, (rows + SUB_ROWS - 1) // SUB_ROWS, 0).astype(jnp.int32)
    base = (cstart * tm_chunk).astype(F32)
    return total.astype(jnp.int32), e.astype(jnp.int32), nsub, base


def _moe_ffn(x_f32, w_router, wg, wu, wd, ln_g, ln_b, *, alpha, splits):
    M, D = x_f32.shape
    n_exp = wg.shape[0]
    avg = TOP_K * M / n_exp
    tm_chunk = SUB_ROWS * max(1, -(-int(avg * 1.11) // SUB_ROWS))
    n_chunk = n_exp + (TOP_K * M) // tm_chunk
    n_slot = n_chunk * tm_chunk

    wr_pad = jnp.pad(w_router, ((0, 0), (0, LANES - n_exp)))
    sel, e12, w12 = _router(x_f32, wr_pad, n_exp=n_exp)
    rank, cnt = _rank(sel)
    n_used, ch_exp, ch_nsub, base = _chunk_tables(cnt[0, :n_exp], n_exp=n_exp, tm_chunk=tm_chunk,
                                                  n_chunk=n_chunk)
    base_row = jnp.pad(base, (0, LANES - n_exp)).reshape(1, LANES)
    pos = _slots(rank, e12, base_row)
    pos1, pos2 = pos[:, 0], pos[:, 1]
    src = _invert(pos1, pos2, n_slot)
    sub_per_chunk = tm_chunk // SUB_ROWS
    tile_valid = (jnp.arange(sub_per_chunk, dtype=jnp.int32)[None, :] < ch_nsub[:, None])
    tile_valid = tile_valid.astype(jnp.int32).reshape(-1)
    xs = _dispatch(x_f32, src, tile_valid)
    hs = _moe_up(xs, wg, wu, n_used, ch_exp, ch_nsub, tm_chunk=tm_chunk, tf=256)
    ys = _moe_down(hs, wd, n_used, ch_exp, ch_nsub, tm_chunk=tm_chunk, tn=1024, tk=1024)
    return _combine(x_f32, ys, pos1, pos2, w12, ln_g, ln_b, alpha=alpha, splits=splits)


def _last_rows(x2d, batch, seq, n, col0, col1):
    return jnp.stack([lax.slice(x2d, (b * seq + seq - n, col0), ((b + 1) * seq, col1)) for b in range(batch)])


def kernel(x_prompt, x_sample, state_rglru_h, state_rglru_conv, state_pool, rg_w_in, rg_conv_w, rg_conv_b,
           rg_w_a, rg_b_a, rg_w_x, rg_b_x, rg_lambda, rg_w_out, pool_w, pool_b, pool_scale, ffn_w_gate,
           ffn_w_up, ffn_w_down, moe_router, moe_w_gate, moe_w_up, moe_w_down, ln_mix_g, ln_mix_b,
           ln_ffn_g, ln_ffn_b):
    B, S, D = x_prompt.shape
    Bs, Ss, _ = x_sample.shape
    depth = ln_mix_g.shape[0]
    alpha = float((2 * depth) ** 0.25)
    d_rnn = rg_w_a.shape[1] * rg_w_a.shape[2]
    gb = rg_w_a.shape[2]
    n_pool_groups = pool_w.shape[1]
    windows = tuple(2 ** (i + 1) for i in range(n_pool_groups))
    pool_buf = state_pool.shape[2]
    conv_w = rg_conv_w.shape[1]
    Mp, Ms = B * S, Bs * Ss

    xp2 = x_prompt.reshape(Mp, D)
    xs2 = jnp.swapaxes(x_sample, 0, 1).reshape(Ms, D)
    x_parts = [xp2, xs2]
    xb = _cast_rows(x_parts)

    new_h_p, new_conv_p, new_pool_p = [], [], []
    new_h_s, new_conv_s, new_pool_s = [], [], []
    y_parts = None
    for i in range(depth):
        j = i // 2
        last = i == depth - 1
        if i % 2 == 0:
            u = _mm_fullk([xb], [rg_w_in[j]], kind="gelu_split", out_dtype=F32, tm=1024, tn=512,
                          n_gelu_cols=d_rnn, name="rg_in_proj")
            wax = jnp.concatenate([rg_w_a[j], rg_w_x[j]], axis=-1).astype(BF16)
            vec = lambda v: v.reshape(1, d_rnn)
            args = (rg_conv_w[j], vec(rg_conv_b[j]), wax, vec(rg_b_a[j]), vec(rg_b_x[j]), vec(rg_lambda[j]))
            yg_p, h_p = _rglru_prompt(u, *args, batch=B, seq=S, d_rnn=d_rnn, gb=gb)
            cs3 = jnp.swapaxes(state_rglru_conv[j], 0, 1)
            yg_s, h_s = _rglru_sample(u, Mp, Ss, cs3, state_rglru_h[j], *args, gb=gb)
            new_h_p.append(h_p)
            new_h_s.append(h_s)
            new_conv_p.append(_last_rows(u, B, S, conv_w - 1, d_rnn, 2 * d_rnn))
            n_new = min(Ss, conv_w - 1)
            tail = lax.slice(u, (Mp + (Ss - n_new) * Bs, d_rnn), (Mp + Ms, 2 * d_rnn)).reshape(n_new, Bs, d_rnn)
            hist = tail if n_new == conv_w - 1 else jnp.concatenate([cs3[n_new:], tail], axis=0)
            new_conv_s.append(jnp.swapaxes(hist, 0, 1))
            mix = _mm_fullk([yg_p, yg_s.reshape(Ms, d_rnn)], [rg_w_out[j]], kind="plain", out_dtype=F32,
                            tm=1024, tn=512, name="rg_out_proj")
            x, xb = _deepnorm(x_parts, mix, ln_mix_g[i], ln_mix_b[i], alpha=alpha, want_bf16=True, name="ln_mix")
            hid = _mm_fullk([xb], [ffn_w_gate[j], ffn_w_up[j]], kind="swiglu", out_dtype=BF16, tm=512, tn=512,
                            name="ffn_up")
            ffn = _mm_ktiled(hid, ffn_w_down[j], tm=2304, tn=1024, tk=1024, name="ffn_down")
            x, xb = _deepnorm([x], ffn, ln_ffn_g[i], ln_ffn_b[i], alpha=alpha, want_bf16=False, name="ln_ffn")
            x_parts = [x]
        else:
            if len(x_parts) != 1:
                x = jnp.concatenate(x_parts, axis=0)
            pw = pool_w[j].astype(BF16)
            vec = lambda v: v.reshape(1, D)
            pargs = (pw, vec(pool_b[j]), vec(pool_scale[j]), vec(ln_mix_g[i]), vec(ln_mix_b[i]))
            x_s3 = lax.slice(x, (Mp, 0), (Mp + Ms, D)).reshape(Ss, Bs, D)
            st3 = jnp.swapaxes(state_pool[j], 0, 1)
            xm = _pool_prompt(x, *pargs, batch=B, seq=S, windows=windows, alpha=alpha)
            xs_f = _pool_sample(st3, x_s3, *pargs, windows=windows, pos0=PAST_LEN, alpha=alpha)
            new_pool_p.append(_last_rows(x, B, S, pool_buf, 0, D))
            new_pool_s.append(jnp.swapaxes(jnp.concatenate([st3, x_s3], axis=0)[-pool_buf:], 0, 1))
            xm = lax.dynamic_update_slice(xm, xs_f.reshape(Ms, D), (Mp, 0))
            outs = _moe_ffn(xm, moe_router[j], moe_w_gate[j], moe_w_up[j], moe_w_down[j],
                            ln_ffn_g[i], ln_ffn_b[i], alpha=alpha, splits=(Mp, Ms))
            x_parts = list(outs)
            if not last:
                xb = _cast_rows(x_parts)

    if len(x_parts) == 1:
        x_parts = [x_parts[0][:Mp], x_parts[0][Mp:]]
    y_prompt = x_parts[0].reshape(B, S, D)
    y_sample = jnp.swapaxes(x_parts[1].reshape(Ss, Bs, D), 0, 1)
    return (y_prompt, y_sample, jnp.stack(new_h_p), jnp.stack(new_conv_p), jnp.stack(new_pool_p),
            jnp.stack(new_h_s), jnp.stack(new_conv_s), jnp.stack(new_pool_s))
```

```python
import functools
import math

import jax
import jax.numpy as jnp
from jax import lax
from jax.experimental import pallas as pl
from jax.experimental.pallas import tpu as pltpu

F32 = jnp.float32
BF16 = jnp.bfloat16

LRU_C = 8.0
LN_EPS = 1e-5
PAST_LEN = 16384
TOP_K = 2
GELU_C = 0.7978845608028654

LANES = 128
SUBLANES = 8
SUB_ROWS = 256
BIG_SUBS = 4
VMEM_LIMIT = 56 * 1024 * 1024


def _pick(n, target, mult):
    best = None
    d = mult
    while d <= min(n, target):
        if n % d == 0:
            best = d
        d += mult
    return best if best is not None else n


def _cp(sem):
    return pltpu.CompilerParams(dimension_semantics=sem, vmem_limit_bytes=VMEM_LIMIT)


def _dot(a, b):
    return jnp.dot(a, b, preferred_element_type=F32)


def _gelu_tanh(x):
    return 0.5 * x * (1.0 + jnp.tanh(GELU_C * (x + 0.044715 * (x * x * x))))


def _softplus(z):
    return jnp.log1p(jnp.exp(-jnp.abs(z))) + jnp.maximum(z, 0.0)


def _sigmoid_tanh(x):
    return 0.5 + 0.5 * jnp.tanh(0.5 * x)


def _layer_norm_rows(z, g, b):
    mu = jnp.mean(z, axis=-1, keepdims=True)
    zc = z - mu
    var = jnp.mean(zc * zc, axis=-1, keepdims=True)
    return zc * lax.rsqrt(var + LN_EPS) * g + b


def _row_sources(arrs, tm, row_axis):
    specs, ranges = [], []
    lo = 0
    for a in arrs:
        nb = a.shape[0] // tm
        if nb * tm != a.shape[0]:
            raise ValueError("row source not divisible by the row tile")

        def imap(*idx, lo=lo, nb=nb):
            return (jnp.clip(idx[row_axis] - lo, 0, nb - 1), 0)

        specs.append(pl.BlockSpec((tm, a.shape[1]), imap))
        ranges.append((lo, lo + nb))
        lo += nb
    return specs, ranges, lo


def _for_row_source(m, ranges, refs, body):
    if len(refs) == 1:
        body(refs[0])
        return
    for (lo, hi), r in zip(ranges, refs):
        @pl.when((m >= lo) & (m < hi))
        def _(r=r):
            body(r)


def _common_tile(arrs, target, mult):
    return _pick(math.gcd(*[a.shape[0] for a in arrs]), target, mult)


def _cast_kernel(*refs, n_x, ranges):
    o_ref = refs[n_x]

    def body(x_ref):
        o_ref[...] = x_ref[...].astype(BF16)

    _for_row_source(pl.program_id(0), ranges, refs[:n_x], body)


def _cast_rows(xs):
    D = xs[0].shape[1]
    tm = _common_tile(xs, 512, 16)
    x_specs, ranges, n_m = _row_sources(xs, tm, row_axis=0)
    return pl.pallas_call(
        functools.partial(_cast_kernel, n_x=len(xs), ranges=ranges),
        grid=(n_m,),
        in_specs=x_specs,
        out_specs=pl.BlockSpec((tm, D), lambda i: (i, 0)),
        out_shape=jax.ShapeDtypeStruct((n_m * tm, D), BF16),
        compiler_params=_cp(("arbitrary",)),
        name="cast_rows",
    )(*xs)


def _mm_fullk_kernel(*refs, kind, n_x, n_w, ranges, n_gelu_tiles):
    x_refs = refs[:n_x]
    w_refs = refs[n_x:n_x + n_w]
    out_ref = refs[n_x + n_w]
    n = pl.program_id(0)
    m = pl.program_id(1)

    def body(x_ref):
        x = x_ref[...]
        if kind == "swiglu":
            g = _dot(x, w_refs[0][...].astype(BF16))
            u = _dot(x, w_refs[1][...].astype(BF16))
            out_ref[...] = (g * jax.nn.sigmoid(g) * u).astype(out_ref.dtype)
        elif kind == "gelu_split":
            u = _dot(x, w_refs[0][...].astype(BF16))

            @pl.when(n < n_gelu_tiles)
            def _():
                out_ref[...] = _gelu_tanh(u)

            @pl.when(n >= n_gelu_tiles)
            def _():
                out_ref[...] = u
        else:
            out_ref[...] = _dot(x, w_refs[0][...].astype(BF16)).astype(out_ref.dtype)

    _for_row_source(m, ranges, x_refs, body)


def _mm_fullk(xs, ws, *, kind, out_dtype, tm, tn, n_gelu_cols=0, name):
    K = xs[0].shape[1]
    N = ws[0].shape[1]
    tm = _common_tile(xs, tm, 16)
    tn = _pick(N, tn, LANES)
    x_specs, ranges, n_m = _row_sources(xs, tm, row_axis=1)
    kern = functools.partial(_mm_fullk_kernel, kind=kind, n_x=len(xs), n_w=len(ws), ranges=ranges,
                             n_gelu_tiles=n_gelu_cols // tn)
    return pl.pallas_call(
        kern,
        grid=(N // tn, n_m),
        in_specs=x_specs + [pl.BlockSpec((K, tn), lambda n, m: (0, n)) for _ in ws],
        out_specs=pl.BlockSpec((tm, tn), lambda n, m: (m, n)),
        out_shape=jax.ShapeDtypeStruct((n_m * tm, N), out_dtype),
        compiler_params=_cp(("arbitrary", "arbitrary")),
        name=name,
    )(*xs, *ws)


def _mm_ktiled_kernel(x_ref, w_ref, o_ref, *, n_piece):
    k = pl.program_id(2)
    rows = x_ref.shape[0] // n_piece

    @pl.when(k == 0)
    def _():
        for p in range(n_piece):
            o_ref[p * rows:(p + 1) * rows, :] = _dot(x_ref[p * rows:(p + 1) * rows, :], w_ref[...].astype(BF16))

    @pl.when(k > 0)
    def _():
        for p in range(n_piece):
            o_ref[p * rows:(p + 1) * rows, :] += _dot(x_ref[p * rows:(p + 1) * rows, :], w_ref[...].astype(BF16))


def _mm_ktiled(x, w, *, tm, tn, tk, name):
    M, K = x.shape
    N = w.shape[1]
    tm = _pick(M, tm, 16)
    tn = _pick(N, tn, LANES)
    tk = _pick(K, tk, LANES)
    n_piece = 3 if tm % (3 * 16) == 0 else 1
    return pl.pallas_call(
        functools.partial(_mm_ktiled_kernel, n_piece=n_piece),
        grid=(M // tm, N // tn, K // tk),
        in_specs=[pl.BlockSpec((tm, tk), lambda m, n, k: (m, k)),
                  pl.BlockSpec((tk, tn), lambda m, n, k: (k, n))],
        out_specs=pl.BlockSpec((tm, tn), lambda m, n, k: (m, n)),
        out_shape=jax.ShapeDtypeStruct((M, N), F32),
        compiler_params=_cp(("arbitrary", "arbitrary", "arbitrary")),
        name=name,
    )(x, w)


def _ln_kernel(*refs, alpha, n_x, ranges, want_bf16):
    x_refs = refs[:n_x]
    y_ref, g_ref, b_ref, of_ref = refs[n_x:n_x + 4]
    i = pl.program_id(0)

    def body(x_ref):
        o = _layer_norm_rows(alpha * x_ref[...] + y_ref[...], g_ref[...], b_ref[...])
        of_ref[...] = o
        if want_bf16:
            refs[n_x + 4][...] = o.astype(BF16)

    _for_row_source(i, ranges, x_refs, body)


def _deepnorm(xs, y, g, b, *, alpha, want_bf16, name):
    M, D = y.shape
    tm = _common_tile(xs, 256, 16)
    x_specs, ranges, n_m = _row_sources(xs, tm, row_axis=0)
    row = pl.BlockSpec((tm, D), lambda i: (i, 0))
    vec = pl.BlockSpec((1, D), lambda i: (0, 0))
    out_specs = [row, row] if want_bf16 else [row]
    out_shape = [jax.ShapeDtypeStruct((M, D), F32)] + ([jax.ShapeDtypeStruct((M, D), BF16)] if want_bf16 else [])
    outs = pl.pallas_call(
        functools.partial(_ln_kernel, alpha=alpha, n_x=len(xs), ranges=ranges, want_bf16=want_bf16),
        grid=(n_m,),
        in_specs=x_specs + [row, vec, vec],
        out_specs=out_specs,
        out_shape=out_shape,
        compiler_params=_cp(("arbitrary",)),
        name=name,
    )(*xs, y, g.reshape(1, D), b.reshape(1, D))
    return (outs[0], outs[1]) if want_bf16 else (outs[0], None)


def _rglru_coeffs(xc, wax_ref, ba, bx, lam, gb):
    xcb = xc.astype(BF16)
    nblk = xc.shape[1] // gb
    r_parts, i_parts = [], []
    for j in range(nblk):
        ru = _dot(xcb[:, j * gb:(j + 1) * gb], wax_ref[j])
        r_parts.append(ru[:, :gb])
        i_parts.append(ru[:, gb:])
    r = _sigmoid_tanh(jnp.concatenate(r_parts, axis=1) + ba)
    ig = _sigmoid_tanh(jnp.concatenate(i_parts, axis=1) + bx)
    log_a = (-LRU_C * r) * _softplus(-lam)
    t = jnp.tanh(0.5 * log_a)
    q = 1.0 / (1.0 - t)
    a = (1.0 + t) * q
    one_minus_a2 = (-4.0 * t) * (q * q)
    b = jnp.sqrt(one_minus_a2) * (ig * xc)
    return a, b


def _rglru_prompt_kernel(gate_ref, rnn_ref, cw_ref, cb_ref, wax_ref, ba_ref, bx_ref, lam_ref,
                         yg_ref, hlast_ref, xp_scr, h_scr, a_scr, b_scr, y_scr, *, tc, gb, cw):
    b_idx = pl.program_id(1)
    i = pl.program_id(2)
    nt = pl.num_programs(2)
    halo = SUBLANES

    @pl.when(i == 0)
    def _():
        xp_scr[0:halo, :] = jnp.zeros((halo, xp_scr.shape[1]), F32)
        h_scr[...] = jnp.zeros(h_scr.shape, F32)

    xp_scr[halo:halo + tc, :] = rnn_ref[...]
    xc = cb_ref[...]
    for k in range(cw):
        xc = xc + xp_scr[pl.ds(halo - (cw - 1) + k, tc), :] * cw_ref[k:k + 1, :]
    a, b = _rglru_coeffs(xc, wax_ref, ba_ref[...], bx_ref[...], lam_ref[...], gb)
    a_scr[...] = a
    b_scr[...] = b

    def step(t, h):
        h = a_scr[pl.ds(t, 1), :] * h + b_scr[pl.ds(t, 1), :]
        y_scr[pl.ds(t, 1), :] = h
        return h

    h = lax.fori_loop(0, tc, step, h_scr[0:1, :], unroll=8)
    h_scr[0:1, :] = h
    yg_ref[...] = (y_scr[...] * gate_ref[...]).astype(BF16)
    xp_scr[0:halo, :] = xp_scr[tc:tc + halo, :]

    @pl.when(i == nt - 1)
    def _():
        hlast_ref[pl.ds(b_idx, 1), :] = h


def _rglru_prompt(u, cwt, cbias, wax, ba, bx, lam, *, batch, seq, d_rnn, gb):
    C = _pick(d_rnn, 1024, gb)
    tc = _pick(seq, 256, SUBLANES)
    nt = seq // tc
    nc = d_rnn // C
    cw = cwt.shape[0]
    vec = pl.BlockSpec((1, C), lambda c, b, i: (0, c))
    kern = functools.partial(_rglru_prompt_kernel, tc=tc, gb=gb, cw=cw)
    return pl.pallas_call(
        kern,
        grid=(nc, batch, nt),
        in_specs=[pl.BlockSpec((tc, C), lambda c, b, i: (b * nt + i, c)),
                  pl.BlockSpec((tc, C), lambda c, b, i: (b * nt + i, nc + c)),
                  pl.BlockSpec((cw, C), lambda c, b, i: (0, c)),
                  vec,
                  pl.BlockSpec((C // gb, gb, 2 * gb), lambda c, b, i: (c, 0, 0)),
                  vec, vec, vec],
        out_specs=[pl.BlockSpec((tc, C), lambda c, b, i: (b * nt + i, c)),
                   pl.BlockSpec((batch, C), lambda c, b, i: (0, c))],
        out_shape=[jax.ShapeDtypeStruct((batch * seq, d_rnn), BF16),
                   jax.ShapeDtypeStruct((batch, d_rnn), F32)],
        scratch_shapes=[pltpu.VMEM((tc + SUBLANES, C), F32), pltpu.VMEM((SUBLANES, C), F32),
                        pltpu.VMEM((tc, C), F32), pltpu.VMEM((tc, C), F32), pltpu.VMEM((tc, C), F32)],
        compiler_params=_cp(("arbitrary", "arbitrary", "arbitrary")),
        name="rglru_prompt",
    )(u, u, cwt, cbias, wax, ba, bx, lam)


def _rglru_sample_kernel(gate_ref, rnn_ref, cs_ref, h0_ref, cw_ref, cb_ref, wax_ref, ba_ref, bx_ref,
                         lam_ref, yg_ref, hlast_ref, xc_scr, *, steps, bsz, gb, cw):
    slabs = [cs_ref[k] for k in range(cw - 1)] + [rnn_ref[t * bsz:(t + 1) * bsz, :] for t in range(steps)]
    for t in range(steps):
        xc = cb_ref[...]
        for k in range(cw):
            xc = xc + slabs[t + k] * cw_ref[k:k + 1, :]
        xc_scr[t * bsz:(t + 1) * bsz, :] = xc
    a, b = _rglru_coeffs(xc_scr[...], wax_ref, ba_ref[...], bx_ref[...], lam_ref[...], gb)
    h = h0_ref[...]
    for t in range(steps):
        h = a[t * bsz:(t + 1) * bsz, :] * h + b[t * bsz:(t + 1) * bsz, :]
        yg_ref[t] = (h * gate_ref[t * bsz:(t + 1) * bsz, :]).astype(BF16)
    hlast_ref[...] = h


def _rglru_sample(u, row0, steps, cs3, h0, cwt, cbias, wax, ba, bx, lam, *, gb):
    bsz, d_rnn = h0.shape
    rows = steps * bsz
    if row0 % rows:
        u = lax.slice(u, (row0, 0), (row0 + rows, u.shape[1]))
        row0 = 0
    rblk = row0 // rows
    C = _pick(d_rnn, 1024, gb)
    nc = d_rnn // C
    cw = cwt.shape[0]
    vec = pl.BlockSpec((1, C), lambda c: (0, c))
    kern = functools.partial(_rglru_sample_kernel, steps=steps, bsz=bsz, gb=gb, cw=cw)
    return pl.pallas_call(
        kern,
        grid=(nc,),
        in_specs=[pl.BlockSpec((rows, C), lambda c: (rblk, c)),
                  pl.BlockSpec((rows, C), lambda c: (rblk, nc + c)),
                  pl.BlockSpec((cw - 1, bsz, C), lambda c: (0, 0, c)),
                  pl.BlockSpec((bsz, C), lambda c: (0, c)),
                  pl.BlockSpec((cw, C), lambda c: (0, c)),
                  vec,
                  pl.BlockSpec((C // gb, gb, 2 * gb), lambda c: (c, 0, 0)),
                  vec, vec, vec],
        out_specs=[pl.BlockSpec((steps, bsz, C), lambda c: (0, 0, c)),
                   pl.BlockSpec((bsz, C), lambda c: (0, c))],
        out_shape=[jax.ShapeDtypeStruct((steps, bsz, d_rnn), BF16),
                   jax.ShapeDtypeStruct((bsz, d_rnn), F32)],
        scratch_shapes=[pltpu.VMEM((steps * bsz, C), F32)],
        compiler_params=_cp(("arbitrary",)),
        name="rglru_sample",
    )(u, u, cs3, h0, cwt, cbias, wax, ba, bx, lam)


def _pool_prompt_kernel(x_ref, pw_ref, pb_ref, ps_ref, g_ref, b_ref, of_ref, xp_scr, z_scr,
                        *, tm, batch, windows, alpha):
    bb = pl.program_id(0)
    i = pl.program_id(1)
    halo = 2 * SUBLANES
    D = x_ref.shape[1]
    pg = D // len(windows)

    @pl.when(bb < batch)
    def _():
        @pl.when(i == 0)
        def _():
            xp_scr[0:halo, :] = jnp.zeros((halo, D), F32)

        xp_scr[halo:halo + tm, :] = x_ref[...]
        pos = i * tm + lax.broadcasted_iota(jnp.int32, (tm, 1), 0)
        for gi, w in enumerate(windows):
            cols = slice(gi * pg, (gi + 1) * pg)
            xg = x_ref[:, cols]
            s = xg
            for j in range(1, w):
                s = s + xp_scr[pl.ds(halo - j, tm), cols]
            cnt = jnp.minimum(w, pos + 1).astype(F32)
            mixed = (s / cnt - xg).astype(BF16)
            out = (_dot(mixed, pw_ref[gi]) + pb_ref[:, cols]) * ps_ref[:, cols]
            z_scr[:, cols] = alpha * xg + out
        of_ref[...] = _layer_norm_rows(z_scr[...], g_ref[...], b_ref[...])
        xp_scr[0:halo, :] = xp_scr[tm:tm + halo, :]

    @pl.when(bb >= batch)
    def _():
        of_ref[...] = jnp.zeros(of_ref.shape, F32)


def _pool_prompt(x, pw, pb, ps, g, b, *, batch, seq, windows, alpha):
    M, D = x.shape
    tm = _pick(math.gcd(seq, M - batch * seq) if M > batch * seq else seq, 256, 16)
    nt = seq // tm
    n_extra = (M - batch * seq) // tm
    if n_extra > nt:
        raise ValueError("extra rows must fit in one pass of the time-tile axis")
    ng = len(windows)

    def rmap(bb, i):
        blk = jnp.where(bb < batch, bb * nt + i, batch * nt + jnp.minimum(i, max(n_extra - 1, 0)))
        return (blk, 0)

    row = pl.BlockSpec((tm, D), rmap)
    vec = pl.BlockSpec((1, D), lambda bb, i: (0, 0))
    kern = functools.partial(_pool_prompt_kernel, tm=tm, batch=batch, windows=windows, alpha=alpha)
    return pl.pallas_call(
        kern,
        grid=(batch + (1 if n_extra else 0), nt),
        in_specs=[row, pl.BlockSpec((ng, D // ng, D // ng), lambda bb, i: (0, 0, 0)), vec, vec, vec, vec],
        out_specs=row,
        out_shape=jax.ShapeDtypeStruct((M, D), F32),
        scratch_shapes=[pltpu.VMEM((tm + 2 * SUBLANES, D), F32), pltpu.VMEM((tm, D), F32)],
        compiler_params=_cp(("arbitrary", "arbitrary")),
        name="pool_prompt",
    )(x, pw, pb, ps, g, b)


def _pool_sample_kernel(st_ref, x_ref, pw_ref, pb_ref, ps_ref, g_ref, b_ref, of_ref,
                        mix_scr, z_scr, *, steps, bt, windows, pos0, alpha):
    nbuf = st_ref.shape[0]
    D = x_ref.shape[2]
    pg = D // len(windows)

    def slab(p, cols):
        return st_ref[p, :, cols] if p < nbuf else x_ref[p - nbuf, :, cols]

    for gi, w in enumerate(windows):
        cols = slice(gi * pg, (gi + 1) * pg)
        for t in range(steps):
            xg = x_ref[t, :, cols]
            s = xg
            for j in range(1, w):
                s = s + slab(nbuf + t - j, cols)
            cnt = float(min(w, pos0 + t + 1))
            mix_scr[t * bt:(t + 1) * bt, cols] = (s / cnt - xg).astype(BF16)
            z_scr[t * bt:(t + 1) * bt, cols] = alpha * xg
    for gi in range(len(windows)):
        cols = slice(gi * pg, (gi + 1) * pg)
        out = (_dot(mix_scr[:, cols], pw_ref[gi]) + pb_ref[:, cols]) * ps_ref[:, cols]
        z_scr[:, cols] = z_scr[:, cols] + out
    o = _layer_norm_rows(z_scr[...], g_ref[...], b_ref[...])
    for t in range(steps):
        of_ref[t] = o[t * bt:(t + 1) * bt, :]


def _pool_sample(st3, x3, pw, pb, ps, g, b, *, windows, pos0, alpha):
    nbuf, bsz, D = st3.shape
    steps = x3.shape[0]
    bt = _pick(bsz, 16, 16)
    ng = len(windows)
    vec = pl.BlockSpec((1, D), lambda j: (0, 0))
    blk = pl.BlockSpec((steps, bt, D), lambda j: (0, j, 0))
    kern = functools.partial(_pool_sample_kernel, steps=steps, bt=bt, windows=windows, pos0=pos0, alpha=alpha)
    return pl.pallas_call(
        kern,
        grid=(bsz // bt,),
        in_specs=[pl.BlockSpec((nbuf, bt, D), lambda j: (0, j, 0)), blk,
                  pl.BlockSpec((ng, D // ng, D // ng), lambda j: (0, 0, 0)), vec, vec, vec, vec],
        out_specs=blk,
        out_shape=jax.ShapeDtypeStruct((steps, bsz, D), F32),
        scratch_shapes=[pltpu.VMEM((steps * bt, D), BF16), pltpu.VMEM((steps * bt, D), F32)],
        compiler_params=_cp(("arbitrary",)),
        name="pool_sample",
    )(st3, x3, pw, pb, ps, g, b)


def _router_kernel(x_ref, wr_ref, sel_ref, e12_ref, w12_ref, *, n_exp):
    x = x_ref[...]
    xh = x.astype(BF16)
    xl = (x - xh.astype(F32)).astype(BF16)
    w = wr_ref[...]
    wh = w.astype(BF16)
    wl = (w - wh.astype(F32)).astype(BF16)
    logits = _dot(xh, wh) + (_dot(xh, wl) + _dot(xl, wh))
    lane = lax.broadcasted_iota(jnp.int32, logits.shape, 1)
    neg = jnp.float32(-jnp.inf)
    logits = jnp.where(lane < n_exp, logits, neg)
    m1 = jnp.max(logits, axis=-1, keepdims=True)
    i1 = jnp.min(jnp.where(logits == m1, lane, LANES), axis=-1, keepdims=True)
    rest = jnp.where(lane == i1, neg, logits)
    m2 = jnp.max(rest, axis=-1, keepdims=True)
    i2 = jnp.min(jnp.where(rest == m2, lane, LANES), axis=-1, keepdims=True)
    e = jnp.exp(m2 - m1)
    den = 1.0 + e
    w1 = 1.0 / den
    w2 = e / den
    sel_ref[...] = jnp.where((lane == i1) | (lane == i2), 1.0, 0.0).astype(F32)
    e12_ref[...] = jnp.where(lane == 0, i1, jnp.where(lane == 1, i2, 0))
    w12_ref[...] = jnp.where(lane == 0, w1, jnp.where(lane == 1, w2, 0.0))


def _router(x, wr_pad, *, n_exp):
    M, D = x.shape
    tm = _pick(M, 512, 16)
    row = pl.BlockSpec((tm, LANES), lambda i: (i, 0))
    return pl.pallas_call(
        functools.partial(_router_kernel, n_exp=n_exp),
        grid=(M // tm,),
        in_specs=[pl.BlockSpec((tm, D), lambda i: (i, 0)), pl.BlockSpec((D, LANES), lambda i: (0, 0))],
        out_specs=[row, row, row],
        out_shape=[jax.ShapeDtypeStruct((M, LANES), F32), jax.ShapeDtypeStruct((M, LANES), jnp.int32),
                   jax.ShapeDtypeStruct((M, LANES), F32)],
        compiler_params=_cp(("arbitrary",)),
        name="moe_router",
    )(x, wr_pad)


def _rank_kernel(sel_ref, rank_ref, cnt_ref, carry):
    i = pl.program_id(0)

    @pl.when(i == 0)
    def _():
        carry[...] = jnp.zeros(carry.shape, F32)

    sel = sel_ref[...]
    tm = sel.shape[0]
    r = lax.broadcasted_iota(jnp.int32, (tm, tm), 0)
    c = lax.broadcasted_iota(jnp.int32, (tm, tm), 1)
    tri = jnp.where(c < r, 1.0, 0.0).astype(BF16)
    excl = _dot(tri, sel.astype(BF16)) + carry[0:1, :]
    rank_ref[...] = excl
    tot = excl[tm - 1:tm, :] + sel[tm - 1:tm, :]
    carry[0:1, :] = tot
    cnt_ref[...] = jnp.broadcast_to(tot, cnt_ref.shape)


def _rank(sel):
    M = sel.shape[0]
    tm = _pick(M, 256, 16)
    return pl.pallas_call(
        _rank_kernel,
        grid=(M // tm,),
        in_specs=[pl.BlockSpec((tm, LANES), lambda i: (i, 0))],
        out_specs=[pl.BlockSpec((tm, LANES), lambda i: (i, 0)), pl.BlockSpec((SUBLANES, LANES), lambda i: (0, 0))],
        out_shape=[jax.ShapeDtypeStruct((M, LANES), F32), jax.ShapeDtypeStruct((SUBLANES, LANES), F32)],
        scratch_shapes=[pltpu.VMEM((SUBLANES, LANES), F32)],
        compiler_params=_cp(("arbitrary",)),
        name="moe_rank",
    )(sel)


def _slot_kernel(rank_ref, e12_ref, base_ref, pos_ref):
    lane = lax.broadcasted_iota(jnp.int32, rank_ref.shape, 1)
    slot = rank_ref[...] + base_ref[...]
    e12 = e12_ref[...]
    p1 = jnp.sum(jnp.where(lane == e12[:, 0:1], slot, 0.0), axis=-1, keepdims=True)
    p2 = jnp.sum(jnp.where(lane == e12[:, 1:2], slot, 0.0), axis=-1, keepdims=True)
    pos_ref[...] = jnp.where(lane == 0, p1, jnp.where(lane == 1, p2, 0.0)).astype(jnp.int32)


def _slots(rank, e12, base_row):
    M = rank.shape[0]
    tm = _pick(M, 512, 16)
    row = pl.BlockSpec((tm, LANES), lambda i: (i, 0))
    return pl.pallas_call(
        _slot_kernel,
        grid=(M // tm,),
        in_specs=[row, row, pl.BlockSpec((1, LANES), lambda i: (0, 0))],
        out_specs=row,
        out_shape=jax.ShapeDtypeStruct((M, LANES), jnp.int32),
        compiler_params=_cp(("arbitrary",)),
        name="moe_slots",
    )(rank, e12, base_row)


def _invert_kernel(p1_ref, p2_ref, src_ref, *, n_tok, n_slot):
    def zero(s, c):
        src_ref[s] = 0
        return c

    lax.fori_loop(0, n_slot, zero, 0, unroll=8)

    def put(t, c):
        src_ref[p1_ref[t]] = t
        src_ref[p2_ref[t]] = t
        return c

    lax.fori_loop(0, n_tok, put, 0, unroll=8)


def _invert(p1, p2, n_slot):
    n_tok = p1.shape[0]
    smem = pl.BlockSpec(memory_space=pltpu.SMEM)
    return pl.pallas_call(
        functools.partial(_invert_kernel, n_tok=n_tok, n_slot=n_slot),
        in_specs=[smem, smem],
        out_specs=smem,
        out_shape=jax.ShapeDtypeStruct((n_slot,), jnp.int32),
        name="moe_invert",
    )(p1, p2)


def _row_copy(src_hbm, row, dst, i, sem):
    return pltpu.make_async_copy(src_hbm.at[pl.ds(row, 1), :], dst.at[pl.ds(i, 1), :], sem)


def _tile_wait(src_hbm, dst, sem):
    pltpu.make_async_copy(src_hbm.at[pl.ds(0, dst.shape[0]), :], dst, sem).wait()


def _dispatch_kernel(valid_ref, src_ref, nsrc_ref, x_hbm, out_ref, buf, sem, *, rows):
    q = pl.program_id(0)
    nq = pl.num_programs(0)

    def start(idx_ref, slot):
        def issue(i, c):
            _row_copy(x_hbm, idx_ref[0, 0, i], buf.at[slot], i, sem.at[slot]).start()
            return c

        lax.fori_loop(0, rows, issue, 0, unroll=8)

    @pl.when((q == 0) & (valid_ref[0] > 0))
    def _():
        start(src_ref, 0)

    nxt = jnp.minimum(q + 1, nq - 1)

    @pl.when((q + 1 < nq) & (valid_ref[nxt] > 0))
    def _():
        start(nsrc_ref, (q + 1) % 2)

    slot = q % 2

    @pl.when(valid_ref[q] > 0)
    def _():
        _tile_wait(x_hbm, buf.at[slot], sem.at[slot])
        out_ref[...] = buf[slot].astype(BF16)

    @pl.when(valid_ref[q] == 0)
    def _():
        out_ref[...] = jnp.zeros(out_ref.shape, BF16)


def _dispatch(x, src, tile_valid):
    n_slot = src.shape[0]
    D = x.shape[1]
    rows = SUB_ROWS
    nt = n_slot // rows
    src3 = src.reshape(nt, 1, rows)
    return pl.pallas_call(
        functools.partial(_dispatch_kernel, rows=rows),
        grid_spec=pltpu.PrefetchScalarGridSpec(
            num_scalar_prefetch=1,
            grid=(nt,),
            in_specs=[pl.BlockSpec((1, 1, rows), lambda q, v: (q, 0, 0), memory_space=pltpu.SMEM),
                      pl.BlockSpec((1, 1, rows), lambda q, v: (jnp.minimum(q + 1, nt - 1), 0, 0),
                                   memory_space=pltpu.SMEM),
                      pl.BlockSpec(memory_space=pl.ANY)],
            out_specs=pl.BlockSpec((rows, D), lambda q, v: (q, 0)),
            scratch_shapes=[pltpu.VMEM((2, rows, D), F32), pltpu.SemaphoreType.DMA((2,))],
        ),
        out_shape=jax.ShapeDtypeStruct((n_slot, D), BF16),
        compiler_params=_cp(("arbitrary",)),
        name="moe_dispatch",
    )(tile_valid, src3, src3, x)


def _combine_kernel(p1_ref, p2_ref, np1_ref, np2_ref, x_ref, w12_ref, g_ref, b_ref, y_hbm, *refs,
                    rows, alpha, bounds):
    n_out = len(bounds)
    out_refs = refs[:n_out]
    ybuf, sem = refs[n_out:]
    i = pl.program_id(0)
    n = pl.num_programs(0)

    def start(a_ref, b_ref_, slot):
        def issue(r, c):
            _row_copy(y_hbm, a_ref[0, 0, r], ybuf.at[slot, 0], r, sem.at[slot]).start()
            _row_copy(y_hbm, b_ref_[0, 0, r], ybuf.at[slot, 1], r, sem.at[slot]).start()
            return c

        lax.fori_loop(0, rows, issue, 0, unroll=8)

    @pl.when(i == 0)
    def _():
        start(p1_ref, p2_ref, 0)

    @pl.when(i + 1 < n)
    def _():
        start(np1_ref, np2_ref, (i + 1) % 2)

    slot = i % 2
    _tile_wait(y_hbm, ybuf.at[slot, 0], sem.at[slot])
    _tile_wait(y_hbm, ybuf.at[slot, 1], sem.at[slot])
    w12 = w12_ref[...]
    ffn = w12[:, 0:1] * ybuf[slot, 0] + w12[:, 1:2] * ybuf[slot, 1]
    res = _layer_norm_rows(alpha * x_ref[...] + ffn, g_ref[...], b_ref[...])
    if n_out == 1:
        out_refs[0][...] = res
    else:
        for (lo, hi), o_ref in zip(bounds, out_refs):
            @pl.when((i >= lo) & (i < hi))
            def _(o_ref=o_ref):
                o_ref[...] = res


def _combine(x, y_sorted, pos1, pos2, w12, g, b, *, alpha, splits):
    M, D = x.shape
    rows = _pick(math.gcd(*splits), SUB_ROWS, 16)
    nt = M // rows
    p1 = pos1.reshape(nt, 1, rows)
    p2 = pos2.reshape(nt, 1, rows)
    idx = pl.BlockSpec((1, 1, rows), lambda i: (i, 0, 0), memory_space=pltpu.SMEM)
    nidx = pl.BlockSpec((1, 1, rows), lambda i: (jnp.minimum(i + 1, nt - 1), 0, 0), memory_space=pltpu.SMEM)
    row = pl.BlockSpec((rows, D), lambda i: (i, 0))
    vec = pl.BlockSpec((1, D), lambda i: (0, 0))
    bounds, out_specs, out_shape = [], [], []
    lo = 0
    for s in splits:
        nb = s // rows
        bounds.append((lo, lo + nb))
        out_specs.append(pl.BlockSpec((rows, D), lambda i, lo=lo, nb=nb: (jnp.clip(i - lo, 0, nb - 1), 0)))
        out_shape.append(jax.ShapeDtypeStruct((s, D), F32))
        lo += nb
    return pl.pallas_call(
        functools.partial(_combine_kernel, rows=rows, alpha=alpha, bounds=tuple(bounds)),
        grid=(nt,),
        in_specs=[idx, idx, nidx, nidx, row, pl.BlockSpec((rows, LANES), lambda i: (i, 0)), vec, vec,
                  pl.BlockSpec(memory_space=pl.ANY)],
        out_specs=out_specs,
        out_shape=out_shape,
        scratch_shapes=[pltpu.VMEM((2, 2, rows, D), F32), pltpu.SemaphoreType.DMA((2,))],
        compiler_params=_cp(("arbitrary",)),
        name="moe_combine",
    )(p1, p2, p1, p2, x, w12, g.reshape(1, D), b.reshape(1, D), y_sorted)


def _piece_loop(nsub, big, fn):
    nbig = lax.div(nsub, jnp.int32(big))

    def big_body(p, c):
        fn(pl.multiple_of(p * (big * SUB_ROWS), big * SUB_ROWS), big * SUB_ROWS)
        return c

    lax.fori_loop(0, nbig, big_body, 0)

    def small_body(s, c):
        fn(pl.multiple_of(s * SUB_ROWS, SUB_ROWS), SUB_ROWS)
        return c

    lax.fori_loop(nbig * big, nsub, small_body, 0)


def _zero_tail(ref, nsub, n_sub):
    def fill(s, c):
        r0 = pl.multiple_of(s * SUB_ROWS, SUB_ROWS)
        ref[pl.ds(r0, SUB_ROWS), :] = jnp.zeros((SUB_ROWS, ref.shape[1]), ref.dtype)
        return c

    lax.fori_loop(nsub, n_sub, fill, 0)


def _moe_up_kernel(exp_ref, nsub_ref, x_ref, wg_ref, wu_ref, h_ref, *, n_sub, big):
    nsub = nsub_ref[pl.program_id(0)]

    def piece(r0, rows):
        x = x_ref[pl.ds(r0, rows), :]
        g = _dot(x, wg_ref[...].astype(BF16))
        u = _dot(x, wu_ref[...].astype(BF16))
        h_ref[pl.ds(r0, rows), :] = (g * jax.nn.sigmoid(g) * u).astype(BF16)

    _piece_loop(nsub, big, piece)
    _zero_tail(h_ref, nsub, n_sub)


def _moe_up(xs, wg, wu, n_used, ch_exp, ch_nsub, *, tm_chunk, tf):
    n_slot, D = xs.shape
    F = wg.shape[2]
    tf = _pick(F, tf, LANES)
    n_sub = tm_chunk // SUB_ROWS
    return pl.pallas_call(
        functools.partial(_moe_up_kernel, n_sub=n_sub, big=min(BIG_SUBS, n_sub)),
        grid_spec=pltpu.PrefetchScalarGridSpec(
            num_scalar_prefetch=2,
            grid=(n_used, F // tf),
            in_specs=[pl.BlockSpec((tm_chunk, D), lambda c, f, e, ns: (c, 0), pipeline_mode=pl.Buffered(1)),
                      pl.BlockSpec((None, D, tf), lambda c, f, e, ns: (e[c], 0, f)),
                      pl.BlockSpec((None, D, tf), lambda c, f, e, ns: (e[c], 0, f))],
            out_specs=pl.BlockSpec((tm_chunk, tf), lambda c, f, e, ns: (c, f)),
        ),
        out_shape=jax.ShapeDtypeStruct((n_slot, F), BF16),
        compiler_params=_cp(("arbitrary", "arbitrary")),
        name="moe_up",
    )(ch_exp, ch_nsub, xs, wg, wu)


def _moe_down_kernel(exp_ref, nsub_ref, h_ref, wd_ref, y_ref, *, n_sub, big):
    k = pl.program_id(2)
    nsub = nsub_ref[pl.program_id(0)]

    @pl.when(k == 0)
    def _():
        def first(r0, rows):
            y_ref[pl.ds(r0, rows), :] = _dot(h_ref[pl.ds(r0, rows), :], wd_ref[...].astype(BF16))

        _piece_loop(nsub, big, first)
        _zero_tail(y_ref, nsub, n_sub)

    @pl.when(k > 0)
    def _():
        def acc(r0, rows):
            y_ref[pl.ds(r0, rows), :] += _dot(h_ref[pl.ds(r0, rows), :], wd_ref[...].astype(BF16))

        _piece_loop(nsub, big, acc)


def _moe_down(hs, wd, n_used, ch_exp, ch_nsub, *, tm_chunk, tn, tk):
    n_slot, F = hs.shape
    D = wd.shape[2]
    tn = _pick(D, tn, LANES)
    tk = _pick(F, tk, LANES)
    n_sub = tm_chunk // SUB_ROWS
    return pl.pallas_call(
        functools.partial(_moe_down_kernel, n_sub=n_sub, big=min(BIG_SUBS, n_sub)),
        grid_spec=pltpu.PrefetchScalarGridSpec(
            num_scalar_prefetch=2,
            grid=(n_used, D // tn, F // tk),
            in_specs=[pl.BlockSpec((tm_chunk, tk), lambda c, n, k, e, ns: (c, k)),
                      pl.BlockSpec((None, tk, tn), lambda c, n, k, e, ns: (e[c], k, n))],
            out_specs=pl.BlockSpec((tm_chunk, tn), lambda c, n, k, e, ns: (c, n)),
        ),
        out_shape=jax.ShapeDtypeStruct((n_slot, D), F32),
        compiler_params=_cp(("arbitrary", "arbitrary", "arbitrary")),
        name="moe_down",
    )(ch_exp, ch_nsub, hs, wd)


def _chunk_tables(cnt, *, n_exp, tm_chunk, n_chunk):
    cnt = cnt.astype(jnp.int32)
    nch_e = (cnt + tm_chunk - 1) // tm_chunk
    cend = jnp.cumsum(nch_e)
    cstart = cend - nch_e
    total = cend[-1]
    j = jnp.arange(n_chunk, dtype=jnp.int32)
    jj = jnp.minimum(j, total - 1)
    e = jnp.minimum(jnp.sum((jj[:, None] >= cend[None, :]).astype(jnp.int32), axis=1), n_exp - 1)
    local = jj - cstart[e]
    rows = jnp.clip(cnt[e] - local * tm_chunk, 0, tm_chunk)
    nsub = jnp.where(j < total, (rows + SUB_ROWS - 1) // SUB_ROWS, 0).astype(jnp.int32)
    base = (cstart * tm_chunk).astype(F32)
    return total.astype(jnp.int32), e.astype(jnp.int32), nsub, base


def _moe_ffn(x_f32, w_router, wg, wu, wd, ln_g, ln_b, *, alpha, splits):
    M, D = x_f32.shape
    n_exp = wg.shape[0]
    avg = TOP_K * M / n_exp
    tm_chunk = SUB_ROWS * max(1, -(-int(avg * 1.11) // SUB_ROWS))
    n_chunk = n_exp + (TOP_K * M) // tm_chunk
    n_slot = n_chunk * tm_chunk

    wr_pad = jnp.pad(w_router, ((0, 0), (0, LANES - n_exp)))
    sel, e12, w12 = _router(x_f32, wr_pad, n_exp=n_exp)
    rank, cnt = _rank(sel)
    n_used, ch_exp, ch_nsub, base = _chunk_tables(cnt[0, :n_exp], n_exp=n_exp, tm_chunk=tm_chunk,
                                                  n_chunk=n_chunk)
    base_row = jnp.pad(base, (0, LANES - n_exp)).reshape(1, LANES)
    pos = _slots(rank, e12, base_row)
    pos1, pos2 = pos[:, 0], pos[:, 1]
    src = _invert(pos1, pos2, n_slot)
    sub_per_chunk = tm_chunk // SUB_ROWS
    tile_valid = (jnp.arange(sub_per_chunk, dtype=jnp.int32)[None, :] < ch_nsub[:, None])
    tile_valid = tile_valid.astype(jnp.int32).reshape(-1)
    xs = _dispatch(x_f32, src, tile_valid)
    hs = _moe_up(xs, wg, wu, n_used, ch_exp, ch_nsub, tm_chunk=tm_chunk, tf=256)
    ys = _moe_down(hs, wd, n_used, ch_exp, ch_nsub, tm_chunk=tm_chunk, tn=1024, tk=1024)
    return _combine(x_f32, ys, pos1, pos2, w12, ln_g, ln_b, alpha=alpha, splits=splits)


def _last_rows(x2d, batch, seq, n, col0, col1):
    return jnp.stack([lax.slice(x2d, (b * seq + seq - n, col0), ((b + 1) * seq, col1)) for b in range(batch)])


def kernel(x_prompt, x_sample, state_rglru_h, state_rglru_conv, state_pool, rg_w_in, rg_conv_w, rg_conv_b,
           rg_w_a, rg_b_a, rg_w_x, rg_b_x, rg_lambda, rg_w_out, pool_w, pool_b, pool_scale, ffn_w_gate,
           ffn_w_up, ffn_w_down, moe_router, moe_w_gate, moe_w_up, moe_w_down, ln_mix_g, ln_mix_b,
           ln_ffn_g, ln_ffn_b):
    B, S, D = x_prompt.shape
    Bs, Ss, _ = x_sample.shape
    depth = ln_mix_g.shape[0]
    alpha = float((2 * depth) ** 0.25)
    d_rnn = rg_w_a.shape[1] * rg_w_a.shape[2]
    gb = rg_w_a.shape[2]
    n_pool_groups = pool_w.shape[1]
    windows = tuple(2 ** (i + 1) for i in range(n_pool_groups))
    pool_buf = state_pool.shape[2]
    conv_w = rg_conv_w.shape[1]
    Mp, Ms = B * S, Bs * Ss

    xp2 = x_prompt.reshape(Mp, D)
    xs2 = jnp.swapaxes(x_sample, 0, 1).reshape(Ms, D)
    x_parts = [xp2, xs2]
    xb = _cast_rows(x_parts)

    new_h_p, new_conv_p, new_pool_p = [], [], []
    new_h_s, new_conv_s, new_pool_s = [], [], []
    y_parts = None
    for i in range(depth):
        j = i // 2
        last = i == depth - 1
        if i % 2 == 0:
            u = _mm_fullk([xb], [rg_w_in[j]], kind="gelu_split", out_dtype=F32, tm=1024, tn=512,
                          n_gelu_cols=d_rnn, name="rg_in_proj")
            wax = jnp.concatenate([rg_w_a[j], rg_w_x[j]], axis=-1).astype(BF16)
            vec = lambda v: v.reshape(1, d_rnn)
            args = (rg_conv_w[j], vec(rg_conv_b[j]), wax, vec(rg_b_a[j]), vec(rg_b_x[j]), vec(rg_lambda[j]))
            yg_p, h_p = _rglru_prompt(u, *args, batch=B, seq=S, d_rnn=d_rnn, gb=gb)
            cs3 = jnp.swapaxes(state_rglru_conv[j], 0, 1)
            yg_s, h_s = _rglru_sample(u, Mp, Ss, cs3, state_rglru_h[j], *args, gb=gb)
            new_h_p.append(h_p)
            new_h_s.append(h_s)
            new_conv_p.append(_last_rows(u, B, S, conv_w - 1, d_rnn, 2 * d_rnn))
            n_new = min(Ss, conv_w - 1)
            tail = lax.slice(u, (Mp + (Ss - n_new) * Bs, d_rnn), (Mp + Ms, 2 * d_rnn)).reshape(n_new, Bs, d_rnn)
            hist = tail if n_new == conv_w - 1 else jnp.concatenate([cs3[n_new:], tail], axis=0)
            new_conv_s.append(jnp.swapaxes(hist, 0, 1))
            mix = _mm_fullk([yg_p, yg_s.reshape(Ms, d_rnn)], [rg_w_out[j]], kind="plain", out_dtype=F32,
                            tm=1024, tn=512, name="rg_out_proj")
            x, xb = _deepnorm(x_parts, mix, ln_mix_g[i], ln_mix_b[i], alpha=alpha, want_bf16=True, name="ln_mix")
            hid = _mm_fullk([xb], [ffn_w_gate[j], ffn_w_up[j]], kind="swiglu", out_dtype=BF16, tm=512, tn=512,
                            name="ffn_up")
            ffn = _mm_ktiled(hid, ffn_w_down[j], tm=2304, tn=1024, tk=1024, name="ffn_down")
            x, xb = _deepnorm([x], ffn, ln_ffn_g[i], ln_ffn_b[i], alpha=alpha, want_bf16=False, name="ln_ffn")
            x_parts = [x]
        else:
            if len(x_parts) != 1:
                x = jnp.concatenate(x_parts, axis=0)
            pw = pool_w[j].astype(BF16)
            vec = lambda v: v.reshape(1, D)
            pargs = (pw, vec(pool_b[j]), vec(pool_scale[j]), vec(ln_mix_g[i]), vec(ln_mix_b[i]))
            x_s3 = lax.slice(x, (Mp, 0), (Mp + Ms, D)).reshape(Ss, Bs, D)
            st3 = jnp.swapaxes(state_pool[j], 0, 1)
            xm = _pool_prompt(x, *pargs, batch=B, seq=S, windows=windows, alpha=alpha)
            xs_f = _pool_sample(st3, x_s3, *pargs, windows=windows, pos0=PAST_LEN, alpha=alpha)
            new_pool_p.append(_last_rows(x, B, S, pool_buf, 0, D))
            new_pool_s.append(jnp.swapaxes(jnp.concatenate([st3, x_s3], axis=0)[-pool_buf:], 0, 1))
            xm = lax.dynamic_update_slice(xm, xs_f.reshape(Ms, D), (Mp, 0))
            outs = _moe_ffn(xm, moe_router[j], moe_w_gate[j], moe_w_up[j], moe_w_down[j],
                            ln_ffn_g[i], ln_ffn_b[i], alpha=alpha, splits=(Mp, Ms))
            x_parts = list(outs)
            if not last:
                xb = _cast_rows(x_parts)

    if len(x_parts) == 1:
        x_parts = [x_parts[0][:Mp], x_parts[0][Mp:]]
    y_prompt = x_parts[0].reshape(B, S, D)
    y_sample = jnp.swapaxes(x_parts[1].reshape(Ss, Bs, D), 0, 1)
    return (y_prompt, y_sample, jnp.stack(new_h_p), jnp.stack(new_conv_p), jnp.stack(new_pool_p),
            jnp.stack(new_h_s), jnp.stack(new_conv_s), jnp.stack(new_pool_s))
```

```python
import functools
import math

import jax
import jax.numpy as jnp
from jax import lax
from jax.experimental import pallas as pl
from jax.experimental.pallas import tpu as pltpu

F32 = jnp.float32
BF16 = jnp.bfloat16

LRU_C = 8.0
LN_EPS = 1e-5
PAST_LEN = 16384
TOP_K = 2
GELU_C = 0.7978845608028654

LANES = 128
SUBLANES = 8
SUB_ROWS = 256
BIG_SUBS = 4
VMEM_LIMIT = 56 * 1024 * 1024


def _pick(n, target, mult):
    best = None
    d = mult
    while d <= min(n, target):
        if n % d == 0:
            best = d
        d += mult
    return best if best is not None else n


def _cp(sem):
    return pltpu.CompilerParams(dimension_semantics=sem, vmem_limit_bytes=VMEM_LIMIT)


def _dot(a, b):
    return jnp.dot(a, b, preferred_element_type=F32)


def _gelu_tanh(x):
    return 0.5 * x * (1.0 + jnp.tanh(GELU_C * (x + 0.044715 * (x * x * x))))


def _softplus(z):
    return jnp.log1p(jnp.exp(-jnp.abs(z))) + jnp.maximum(z, 0.0)


def _sigmoid_tanh(x):
    return 0.5 + 0.5 * jnp.tanh(0.5 * x)


def _layer_norm_rows(z, g, b):
    mu = jnp.mean(z, axis=-1, keepdims=True)
    zc = z - mu
    var = jnp.mean(zc * zc, axis=-1, keepdims=True)
    return zc * lax.rsqrt(var + LN_EPS) * g + b


def _row_sources(arrs, tm, row_axis):
    specs, ranges = [], []
    lo = 0
    for a in arrs:
        nb = a.shape[0] // tm
        if nb * tm != a.shape[0]:
            raise ValueError("row source not divisible by the row tile")

        def imap(*idx, lo=lo, nb=nb):
            return (jnp.clip(idx[row_axis] - lo, 0, nb - 1), 0)

        specs.append(pl.BlockSpec((tm, a.shape[1]), imap))
        ranges.append((lo, lo + nb))
        lo += nb
    return specs, ranges, lo


def _for_row_source(m, ranges, refs, body):
    if len(refs) == 1:
        body(refs[0])
        return
    for (lo, hi), r in zip(ranges, refs):
        @pl.when((m >= lo) & (m < hi))
        def _(r=r):
            body(r)


def _common_tile(arrs, target, mult):
    return _pick(math.gcd(*[a.shape[0] for a in arrs]), target, mult)


def _cast_kernel(*refs, n_x, ranges):
    o_ref = refs[n_x]

    def body(x_ref):
        o_ref[...] = x_ref[...].astype(BF16)

    _for_row_source(pl.program_id(0), ranges, refs[:n_x], body)


def _cast_rows(xs):
    D = xs[0].shape[1]
    tm = _common_tile(xs, 512, 16)
    x_specs, ranges, n_m = _row_sources(xs, tm, row_axis=0)
    return pl.pallas_call(
        functools.partial(_cast_kernel, n_x=len(xs), ranges=ranges),
        grid=(n_m,),
        in_specs=x_specs,
        out_specs=pl.BlockSpec((tm, D), lambda i: (i, 0)),
        out_shape=jax.ShapeDtypeStruct((n_m * tm, D), BF16),
        compiler_params=_cp(("arbitrary",)),
        name="cast_rows",
    )(*xs)


def _mm_fullk_kernel(*refs, kind, n_x, n_w, ranges, n_gelu_tiles):
    x_refs = refs[:n_x]
    w_refs = refs[n_x:n_x + n_w]
    out_ref = refs[n_x + n_w]
    n = pl.program_id(0)
    m = pl.program_id(1)

    def body(x_ref):
        x = x_ref[...]
        if kind == "swiglu":
            g = _dot(x, w_refs[0][...].astype(BF16))
            u = _dot(x, w_refs[1][...].astype(BF16))
            out_ref[...] = (g * jax.nn.sigmoid(g) * u).astype(out_ref.dtype)
        elif kind == "gelu_split":
            u = _dot(x, w_refs[0][...].astype(BF16))

            @pl.when(n < n_gelu_tiles)
            def _():
                out_ref[...] = _gelu_tanh(u)

            @pl.when(n >= n_gelu_tiles)
            def _():
                out_ref[...] = u
        else:
            out_ref[...] = _dot(x, w_refs[0][...].astype(BF16)).astype(out_ref.dtype)

    _for_row_source(m, ranges, x_refs, body)


def _mm_fullk(xs, ws, *, kind, out_dtype, tm, tn, n_gelu_cols=0, name):
    K = xs[0].shape[1]
    N = ws[0].shape[1]
    tm = _common_tile(xs, tm, 16)
    tn = _pick(N, tn, LANES)
    x_specs, ranges, n_m = _row_sources(xs, tm, row_axis=1)
    kern = functools.partial(_mm_fullk_kernel, kind=kind, n_x=len(xs), n_w=len(ws), ranges=ranges,
                             n_gelu_tiles=n_gelu_cols // tn)
    return pl.pallas_call(
        kern,
        grid=(N // tn, n_m),
        in_specs=x_specs + [pl.BlockSpec((K, tn), lambda n, m: (0, n)) for _ in ws],
        out_specs=pl.BlockSpec((tm, tn), lambda n, m: (m, n)),
        out_shape=jax.ShapeDtypeStruct((n_m * tm, N), out_dtype),
        compiler_params=_cp(("arbitrary", "arbitrary")),
        name=name,
    )(*xs, *ws)


def _mm_ktiled_kernel(x_ref, w_ref, o_ref, *, n_piece):
    k = pl.program_id(2)
    rows = x_ref.shape[0] // n_piece

    @pl.when(k == 0)
    def _():
        for p in range(n_piece):
            o_ref[p * rows:(p + 1) * rows, :] = _dot(x_ref[p * rows:(p + 1) * rows, :], w_ref[...].astype(BF16))

    @pl.when(k > 0)
    def _():
        for p in range(n_piece):
            o_ref[p * rows:(p + 1) * rows, :] += _dot(x_ref[p * rows:(p + 1) * rows, :], w_ref[...].astype(BF16))


def _mm_ktiled(x, w, *, tm, tn, tk, name):
    M, K = x.shape
    N = w.shape[1]
    tm = _pick(M, tm, 16)
    tn = _pick(N, tn, LANES)
    tk = _pick(K, tk, LANES)
    n_piece = 3 if tm % (3 * 16) == 0 else 1
    return pl.pallas_call(
        functools.partial(_mm_ktiled_kernel, n_piece=n_piece),
        grid=(M // tm, N // tn, K // tk),
        in_specs=[pl.BlockSpec((tm, tk), lambda m, n, k: (m, k)),
                  pl.BlockSpec((tk, tn), lambda m, n, k: (k, n))],
        out_specs=pl.BlockSpec((tm, tn), lambda m, n, k: (m, n)),
        out_shape=jax.ShapeDtypeStruct((M, N), F32),
        compiler_params=_cp(("arbitrary", "arbitrary", "arbitrary")),
        name=name,
    )(x, w)


def _ln_kernel(*refs, alpha, n_x, ranges, want_bf16):
    x_refs = refs[:n_x]
    y_ref, g_ref, b_ref, of_ref = refs[n_x:n_x + 4]
    i = pl.program_id(0)

    def body(x_ref):
        o = _layer_norm_rows(alpha * x_ref[...] + y_ref[...], g_ref[...], b_ref[...])
        of_ref[...] = o
        if want_bf16:
            refs[n_x + 4][...] = o.astype(BF16)

    _for_row_source(i, ranges, x_refs, body)


def _deepnorm(xs, y, g, b, *, alpha, want_bf16, name):
    M, D = y.shape
    tm = _common_tile(xs, 256, 16)
    x_specs, ranges, n_m = _row_sources(xs, tm, row_axis=0)
    row = pl.BlockSpec((tm, D), lambda i: (i, 0))
    vec = pl.BlockSpec((1, D), lambda i: (0, 0))
    out_specs = [row, row] if want_bf16 else [row]
    out_shape = [jax.ShapeDtypeStruct((M, D), F32)] + ([jax.ShapeDtypeStruct((M, D), BF16)] if want_bf16 else [])
    outs = pl.pallas_call(
        functools.partial(_ln_kernel, alpha=alpha, n_x=len(xs), ranges=ranges, want_bf16=want_bf16),
        grid=(n_m,),
        in_specs=x_specs + [row, vec, vec],
        out_specs=out_specs,
        out_shape=out_shape,
        compiler_params=_cp(("arbitrary",)),
        name=name,
    )(*xs, y, g.reshape(1, D), b.reshape(1, D))
    return (outs[0], outs[1]) if want_bf16 else (outs[0], None)


def _rglru_coeffs(xc, wax_ref, ba, bx, lam, gb):
    xcb = xc.astype(BF16)
    nblk = xc.shape[1] // gb
    r_parts, i_parts = [], []
    for j in range(nblk):
        ru = _dot(xcb[:, j * gb:(j + 1) * gb], wax_ref[j])
        r_parts.append(ru[:, :gb])
        i_parts.append(ru[:, gb:])
    r = _sigmoid_tanh(jnp.concatenate(r_parts, axis=1) + ba)
    ig = _sigmoid_tanh(jnp.concatenate(i_parts, axis=1) + bx)
    log_a = (-LRU_C * r) * _softplus(-lam)
    t = jnp.tanh(0.5 * log_a)
    q = 1.0 / (1.0 - t)
    a = (1.0 + t) * q
    one_minus_a2 = (-4.0 * t) * (q * q)
    b = jnp.sqrt(one_minus_a2) * (ig * xc)
    return a, b


def _rglru_prompt_kernel(gate_ref, rnn_ref, cw_ref, cb_ref, wax_ref, ba_ref, bx_ref, lam_ref,
                         yg_ref, hlast_ref, xp_scr, h_scr, a_scr, b_scr, y_scr, *, tc, gb, cw):
    b_idx = pl.program_id(1)
    i = pl.program_id(2)
    nt = pl.num_programs(2)
    halo = SUBLANES

    @pl.when(i == 0)
    def _():
        xp_scr[0:halo, :] = jnp.zeros((halo, xp_scr.shape[1]), F32)
        h_scr[...] = jnp.zeros(h_scr.shape, F32)

    xp_scr[halo:halo + tc, :] = rnn_ref[...]
    xc = cb_ref[...]
    for k in range(cw):
        xc = xc + xp_scr[pl.ds(halo - (cw - 1) + k, tc), :] * cw_ref[k:k + 1, :]
    a, b = _rglru_coeffs(xc, wax_ref, ba_ref[...], bx_ref[...], lam_ref[...], gb)
    a_scr[...] = a
    b_scr[...] = b

    def step(t, h):
        h = a_scr[pl.ds(t, 1), :] * h + b_scr[pl.ds(t, 1), :]
        y_scr[pl.ds(t, 1), :] = h
        return h

    h = lax.fori_loop(0, tc, step, h_scr[0:1, :], unroll=8)
    h_scr[0:1, :] = h
    yg_ref[...] = (y_scr[...] * gate_ref[...]).astype(BF16)
    xp_scr[0:halo, :] = xp_scr[tc:tc + halo, :]

    @pl.when(i == nt - 1)
    def _():
        hlast_ref[pl.ds(b_idx, 1), :] = h


def _rglru_prompt(u, cwt, cbias, wax, ba, bx, lam, *, batch, seq, d_rnn, gb):
    C = _pick(d_rnn, 1024, gb)
    tc = _pick(seq, 256, SUBLANES)
    nt = seq // tc
    nc = d_rnn // C
    cw = cwt.shape[0]
    vec = pl.BlockSpec((1, C), lambda c, b, i: (0, c))
    kern = functools.partial(_rglru_prompt_kernel, tc=tc, gb=gb, cw=cw)
    return pl.pallas_call(
        kern,
        grid=(nc, batch, nt),
        in_specs=[pl.BlockSpec((tc, C), lambda c, b, i: (b * nt + i, c)),
                  pl.BlockSpec((tc, C), lambda c, b, i: (b * nt + i, nc + c)),
                  pl.BlockSpec((cw, C), lambda c, b, i: (0, c)),
                  vec,
                  pl.BlockSpec((C // gb, gb, 2 * gb), lambda c, b, i: (c, 0, 0)),
                  vec, vec, vec],
        out_specs=[pl.BlockSpec((tc, C), lambda c, b, i: (b * nt + i, c)),
                   pl.BlockSpec((batch, C), lambda c, b, i: (0, c))],
        out_shape=[jax.ShapeDtypeStruct((batch * seq, d_rnn), BF16),
                   jax.ShapeDtypeStruct((batch, d_rnn), F32)],
        scratch_shapes=[pltpu.VMEM((tc + SUBLANES, C), F32), pltpu.VMEM((SUBLANES, C), F32),
                        pltpu.VMEM((tc, C), F32), pltpu.VMEM((tc, C), F32), pltpu.VMEM((tc, C), F32)],
        compiler_params=_cp(("arbitrary", "arbitrary", "arbitrary")),
        name="rglru_prompt",
    )(u, u, cwt, cbias, wax, ba, bx, lam)


def _rglru_sample_kernel(gate_ref, rnn_ref, cs_ref, h0_ref, cw_ref, cb_ref, wax_ref, ba_ref, bx_ref,
                         lam_ref, yg_ref, hlast_ref, xc_scr, *, steps, bsz, gb, cw):
    slabs = [cs_ref[k] for k in range(cw - 1)] + [rnn_ref[t * bsz:(t + 1) * bsz, :] for t in range(steps)]
    for t in range(steps):
        xc = cb_ref[...]
        for k in range(cw):
            xc = xc + slabs[t + k] * cw_ref[k:k + 1, :]
        xc_scr[t * bsz:(t + 1) * bsz, :] = xc
    a, b = _rglru_coeffs(xc_scr[...], wax_ref, ba_ref[...], bx_ref[...], lam_ref[...], gb)
    h = h0_ref[...]
    for t in range(steps):
        h = a[t * bsz:(t + 1) * bsz, :] * h + b[t * bsz:(t + 1) * bsz, :]
        yg_ref[t] = (h * gate_ref[t * bsz:(t + 1) * bsz, :]).astype(BF16)
    hlast_ref[...] = h


def _rglru_sample(u, row0, steps, cs3, h0, cwt, cbias, wax, ba, bx, lam, *, gb):
    bsz, d_rnn = h0.shape
    rows = steps * bsz
    if row0 % rows:
        u = lax.slice(u, (row0, 0), (row0 + rows, u.shape[1]))
        row0 = 0
    rblk = row0 // rows
    C = _pick(d_rnn, 1024, gb)
    nc = d_rnn // C
    cw = cwt.shape[0]
    vec = pl.BlockSpec((1, C), lambda c: (0, c))
    kern = functools.partial(_rglru_sample_kernel, steps=steps, bsz=bsz, gb=gb, cw=cw)
    return pl.pallas_call(
        kern,
        grid=(nc,),
        in_specs=[pl.BlockSpec((rows, C), lambda c: (rblk, c)),
                  pl.BlockSpec((rows, C), lambda c: (rblk, nc + c)),
                  pl.BlockSpec((cw - 1, bsz, C), lambda c: (0, 0, c)),
                  pl.BlockSpec((bsz, C), lambda c: (0, c)),
                  pl.BlockSpec((cw, C), lambda c: (0, c)),
                  vec,
                  pl.BlockSpec((C // gb, gb, 2 * gb), lambda c: (c, 0, 0)),
                  vec, vec, vec],
        out_specs=[pl.BlockSpec((steps, bsz, C), lambda c: (0, 0, c)),
                   pl.BlockSpec((bsz, C), lambda c: (0, c))],
        out_shape=[jax.ShapeDtypeStruct((steps, bsz, d_rnn), BF16),
                   jax.ShapeDtypeStruct((bsz, d_rnn), F32)],
        scratch_shapes=[pltpu.VMEM((steps * bsz, C), F32)],
        compiler_params=_cp(("arbitrary",)),
        name="rglru_sample",
    )(u, u, cs3, h0, cwt, cbias, wax, ba, bx, lam)


def _pool_prompt_kernel(x_ref, pw_ref, pb_ref, ps_ref, g_ref, b_ref, of_ref, xp_scr, z_scr,
                        *, tm, batch, windows, alpha):
    bb = pl.program_id(0)
    i = pl.program_id(1)
    halo = 2 * SUBLANES
    D = x_ref.shape[1]
    pg = D // len(windows)

    @pl.when(bb < batch)
    def _():
        @pl.when(i == 0)
        def _():
            xp_scr[0:halo, :] = jnp.zeros((halo, D), F32)

        xp_scr[halo:halo + tm, :] = x_ref[...]
        pos = i * tm + lax.broadcasted_iota(jnp.int32, (tm, 1), 0)
        for gi, w in enumerate(windows):
            cols = slice(gi * pg, (gi + 1) * pg)
            xg = x_ref[:, cols]
            s = xg
            for j in range(1, w):
                s = s + xp_scr[pl.ds(halo - j, tm), cols]
            cnt = jnp.minimum(w, pos + 1).astype(F32)
            mixed = (s / cnt - xg).astype(BF16)
            out = (_dot(mixed, pw_ref[gi]) + pb_ref[:, cols]) * ps_ref[:, cols]
            z_scr[:, cols] = alpha * xg + out
        of_ref[...] = _layer_norm_rows(z_scr[...], g_ref[...], b_ref[...])
        xp_scr[0:halo, :] = xp_scr[tm:tm + halo, :]

    @pl.when(bb >= batch)
    def _():
        of_ref[...] = jnp.zeros(of_ref.shape, F32)


def _pool_prompt(x, pw, pb, ps, g, b, *, batch, seq, windows, alpha):
    M, D = x.shape
    tm = _pick(math.gcd(seq, M - batch * seq) if M > batch * seq else seq, 256, 16)
    nt = seq // tm
    n_extra = (M - batch * seq) // tm
    if n_extra > nt:
        raise ValueError("extra rows must fit in one pass of the time-tile axis")
    ng = len(windows)

    def rmap(bb, i):
        blk = jnp.where(bb < batch, bb * nt + i, batch * nt + jnp.minimum(i, max(n_extra - 1, 0)))
        return (blk, 0)

    row = pl.BlockSpec((tm, D), rmap)
    vec = pl.BlockSpec((1, D), lambda bb, i: (0, 0))
    kern = functools.partial(_pool_prompt_kernel, tm=tm, batch=batch, windows=windows, alpha=alpha)
    return pl.pallas_call(
        kern,
        grid=(batch + (1 if n_extra else 0), nt),
        in_specs=[row, pl.BlockSpec((ng, D // ng, D // ng), lambda bb, i: (0, 0, 0)), vec, vec, vec, vec],
        out_specs=row,
        out_shape=jax.ShapeDtypeStruct((M, D), F32),
        scratch_shapes=[pltpu.VMEM((tm + 2 * SUBLANES, D), F32), pltpu.VMEM((tm, D), F32)],
        compiler_params=_cp(("arbitrary", "arbitrary")),
        name="pool_prompt",
    )(x, pw, pb, ps, g, b)


def _pool_sample_kernel(st_ref, x_ref, pw_ref, pb_ref, ps_ref, g_ref, b_ref, of_ref,
                        mix_scr, z_scr, *, steps, bt, windows, pos0, alpha):
    nbuf = st_ref.shape[0]
    D = x_ref.shape[2]
    pg = D // len(windows)

    def slab(p, cols):
        return st_ref[p, :, cols] if p < nbuf else x_ref[p - nbuf, :, cols]

    for gi, w in enumerate(windows):
        cols = slice(gi * pg, (gi + 1) * pg)
        for t in range(steps):
            xg = x_ref[t, :, cols]
            s = xg
            for j in range(1, w):
                s = s + slab(nbuf + t - j, cols)
            cnt = float(min(w, pos0 + t + 1))
            mix_scr[t * bt:(t + 1) * bt, cols] = (s / cnt - xg).astype(BF16)
            z_scr[t * bt:(t + 1) * bt, cols] = alpha * xg
    for gi in range(len(windows)):
        cols = slice(gi * pg, (gi + 1) * pg)
        out = (_dot(mix_scr[:, cols], pw_ref[gi]) + pb_ref[:, cols]) * ps_ref[:, cols]
        z_scr[:, cols] = z_scr[:, cols] + out
    o = _layer_norm_rows(z_scr[...], g_ref[...], b_ref[...])
    for t in range(steps):
        of_ref[t] = o[t * bt:(t + 1) * bt, :]


def _pool_sample(st3, x3, pw, pb, ps, g, b, *, windows, pos0, alpha):
    nbuf, bsz, D = st3.shape
    steps = x3.shape[0]
    bt = _pick(bsz, 16, 16)
    ng = len(windows)
    vec = pl.BlockSpec((1, D), lambda j: (0, 0))
    blk = pl.BlockSpec((steps, bt, D), lambda j: (0, j, 0))
    kern = functools.partial(_pool_sample_kernel, steps=steps, bt=bt, windows=windows, pos0=pos0, alpha=alpha)
    return pl.pallas_call(
        kern,
        grid=(bsz // bt,),
        in_specs=[pl.BlockSpec((nbuf, bt, D), lambda j: (0, j, 0)), blk,
                  pl.BlockSpec((ng, D // ng, D // ng), lambda j: (0, 0, 0)), vec, vec, vec, vec],
        out_specs=blk,
        out_shape=jax.ShapeDtypeStruct((steps, bsz, D), F32),
        scratch_shapes=[pltpu.VMEM((steps * bt, D), BF16), pltpu.VMEM((steps * bt, D), F32)],
        compiler_params=_cp(("arbitrary",)),
        name="pool_sample",
    )(st3, x3, pw, pb, ps, g, b)


def _router_kernel(x_ref, wr_ref, sel_ref, e12_ref, w12_ref, *, n_exp):
    x = x_ref[...]
    xh = x.astype(BF16)
    xl = (x - xh.astype(F32)).astype(BF16)
    w = wr_ref[...]
    wh = w.astype(BF16)
    wl = (w - wh.astype(F32)).astype(BF16)
    logits = _dot(xh, wh) + (_dot(xh, wl) + _dot(xl, wh))
    lane = lax.broadcasted_iota(jnp.int32, logits.shape, 1)
    neg = jnp.float32(-jnp.inf)
    logits = jnp.where(lane < n_exp, logits, neg)
    m1 = jnp.max(logits, axis=-1, keepdims=True)
    i1 = jnp.min(jnp.where(logits == m1, lane, LANES), axis=-1, keepdims=True)
    rest = jnp.where(lane == i1, neg, logits)
    m2 = jnp.max(rest, axis=-1, keepdims=True)
    i2 = jnp.min(jnp.where(rest == m2, lane, LANES), axis=-1, keepdims=True)
    e = jnp.exp(m2 - m1)
    den = 1.0 + e
    w1 = 1.0 / den
    w2 = e / den
    sel_ref[...] = jnp.where((lane == i1) | (lane == i2), 1.0, 0.0).astype(F32)
    e12_ref[...] = jnp.where(lane == 0, i1, jnp.where(lane == 1, i2, 0))
    w12_ref[...] = jnp.where(lane == 0, w1, jnp.where(lane == 1, w2, 0.0))


def _router(x, wr_pad, *, n_exp):
    M, D = x.shape
    tm = _pick(M, 512, 16)
    row = pl.BlockSpec((tm, LANES), lambda i: (i, 0))
    return pl.pallas_call(
        functools.partial(_router_kernel, n_exp=n_exp),
        grid=(M // tm,),
        in_specs=[pl.BlockSpec((tm, D), lambda i: (i, 0)), pl.BlockSpec((D, LANES), lambda i: (0, 0))],
        out_specs=[row, row, row],
        out_shape=[jax.ShapeDtypeStruct((M, LANES), F32), jax.ShapeDtypeStruct((M, LANES), jnp.int32),
                   jax.ShapeDtypeStruct((M, LANES), F32)],
        compiler_params=_cp(("arbitrary",)),
        name="moe_router",
    )(x, wr_pad)


def _rank_kernel(sel_ref, rank_ref, cnt_ref, carry):
    i = pl.program_id(0)

    @pl.when(i == 0)
    def _():
        carry[...] = jnp.zeros(carry.shape, F32)

    sel = sel_ref[...]
    tm = sel.shape[0]
    r = lax.broadcasted_iota(jnp.int32, (tm, tm), 0)
    c = lax.broadcasted_iota(jnp.int32, (tm, tm), 1)
    tri = jnp.where(c < r, 1.0, 0.0).astype(BF16)
    excl = _dot(tri, sel.astype(BF16)) + carry[0:1, :]
    rank_ref[...] = excl
    tot = excl[tm - 1:tm, :] + sel[tm - 1:tm, :]
    carry[0:1, :] = tot
    cnt_ref[...] = jnp.broadcast_to(tot, cnt_ref.shape)


def _rank(sel):
    M = sel.shape[0]
    tm = _pick(M, 256, 16)
    return pl.pallas_call(
        _rank_kernel,
        grid=(M // tm,),
        in_specs=[pl.BlockSpec((tm, LANES), lambda i: (i, 0))],
        out_specs=[pl.BlockSpec((tm, LANES), lambda i: (i, 0)), pl.BlockSpec((SUBLANES, LANES), lambda i: (0, 0))],
        out_shape=[jax.ShapeDtypeStruct((M, LANES), F32), jax.ShapeDtypeStruct((SUBLANES, LANES), F32)],
        scratch_shapes=[pltpu.VMEM((SUBLANES, LANES), F32)],
        compiler_params=_cp(("arbitrary",)),
        name="moe_rank",
    )(sel)


def _slot_kernel(rank_ref, e12_ref, base_ref, pos_ref):
    lane = lax.broadcasted_iota(jnp.int32, rank_ref.shape, 1)
    slot = rank_ref[...] + base_ref[...]
    e12 = e12_ref[...]
    p1 = jnp.sum(jnp.where(lane == e12[:, 0:1], slot, 0.0), axis=-1, keepdims=True)
    p2 = jnp.sum(jnp.where(lane == e12[:, 1:2], slot, 0.0), axis=-1, keepdims=True)
    pos_ref[...] = jnp.where(lane == 0, p1, jnp.where(lane == 1, p2, 0.0)).astype(jnp.int32)


def _slots(rank, e12, base_row):
    M = rank.shape[0]
    tm = _pick(M, 512, 16)
    row = pl.BlockSpec((tm, LANES), lambda i: (i, 0))
    return pl.pallas_call(
        _slot_kernel,
        grid=(M // tm,),
        in_specs=[row, row, pl.BlockSpec((1, LANES), lambda i: (0, 0))],
        out_specs=row,
        out_shape=jax.ShapeDtypeStruct((M, LANES), jnp.int32),
        compiler_params=_cp(("arbitrary",)),
        name="moe_slots",
    )(rank, e12, base_row)


def _invert_kernel(p1_ref, p2_ref, src_ref, *, n_tok, n_slot):
    def zero(s, c):
        src_ref[s] = 0
        return c

    lax.fori_loop(0, n_slot, zero, 0, unroll=8)

    def put(t, c):
        src_ref[p1_ref[t]] = t
        src_ref[p2_ref[t]] = t
        return c

    lax.fori_loop(0, n_tok, put, 0, unroll=8)


def _invert(p1, p2, n_slot):
    n_tok = p1.shape[0]
    smem = pl.BlockSpec(memory_space=pltpu.SMEM)
    return pl.pallas_call(
        functools.partial(_invert_kernel, n_tok=n_tok, n_slot=n_slot),
        in_specs=[smem, smem],
        out_specs=smem,
        out_shape=jax.ShapeDtypeStruct((n_slot,), jnp.int32),
        name="moe_invert",
    )(p1, p2)


def _row_copy(src_hbm, row, dst, i, sem):
    return pltpu.make_async_copy(src_hbm.at[pl.ds(row, 1), :], dst.at[pl.ds(i, 1), :], sem)


def _tile_wait(src_hbm, dst, sem):
    pltpu.make_async_copy(src_hbm.at[pl.ds(0, dst.shape[0]), :], dst, sem).wait()


def _dispatch_kernel(valid_ref, src_ref, nsrc_ref, x_hbm, out_ref, buf, sem, *, rows):
    q = pl.program_id(0)
    nq = pl.num_programs(0)

    def start(idx_ref, slot):
        def issue(i, c):
            _row_copy(x_hbm, idx_ref[0, 0, i], buf.at[slot], i, sem.at[slot]).start()
            return c

        lax.fori_loop(0, rows, issue, 0, unroll=8)

    @pl.when((q == 0) & (valid_ref[0] > 0))
    def _():
        start(src_ref, 0)

    nxt = jnp.minimum(q + 1, nq - 1)

    @pl.when((q + 1 < nq) & (valid_ref[nxt] > 0))
    def _():
        start(nsrc_ref, (q + 1) % 2)

    slot = q % 2

    @pl.when(valid_ref[q] > 0)
    def _():
        _tile_wait(x_hbm, buf.at[slot], sem.at[slot])
        out_ref[...] = buf[slot].astype(BF16)

    @pl.when(valid_ref[q] == 0)
    def _():
        out_ref[...] = jnp.zeros(out_ref.shape, BF16)


def _dispatch(x, src, tile_valid):
    n_slot = src.shape[0]
    D = x.shape[1]
    rows = SUB_ROWS
    nt = n_slot // rows
    src3 = src.reshape(nt, 1, rows)
    return pl.pallas_call(
        functools.partial(_dispatch_kernel, rows=rows),
        grid_spec=pltpu.PrefetchScalarGridSpec(
            num_scalar_prefetch=1,
            grid=(nt,),
            in_specs=[pl.BlockSpec((1, 1, rows), lambda q, v: (q, 0, 0), memory_space=pltpu.SMEM),
                      pl.BlockSpec((1, 1, rows), lambda q, v: (jnp.minimum(q + 1, nt - 1), 0, 0),
                                   memory_space=pltpu.SMEM),
                      pl.BlockSpec(memory_space=pl.ANY)],
            out_specs=pl.BlockSpec((rows, D), lambda q, v: (q, 0)),
            scratch_shapes=[pltpu.VMEM((2, rows, D), F32), pltpu.SemaphoreType.DMA((2,))],
        ),
        out_shape=jax.ShapeDtypeStruct((n_slot, D), BF16),
        compiler_params=_cp(("arbitrary",)),
        name="moe_dispatch",
    )(tile_valid, src3, src3, x)


def _combine_kernel(p1_ref, p2_ref, np1_ref, np2_ref, x_ref, w12_ref, g_ref, b_ref, y_hbm, *refs,
                    rows, alpha, bounds):
    n_out = len(bounds)
    out_refs = refs[:n_out]
    ybuf, sem = refs[n_out:]
    i = pl.program_id(0)
    n = pl.num_programs(0)

    def start(a_ref, b_ref_, slot):
        def issue(r, c):
            _row_copy(y_hbm, a_ref[0, 0, r], ybuf.at[slot, 0], r, sem.at[slot]).start()
            _row_copy(y_hbm, b_ref_[0, 0, r], ybuf.at[slot, 1], r, sem.at[slot]).start()
            return c

        lax.fori_loop(0, rows, issue, 0, unroll=8)

    @pl.when(i == 0)
    def _():
        start(p1_ref, p2_ref, 0)

    @pl.when(i + 1 < n)
    def _():
        start(np1_ref, np2_ref, (i + 1) % 2)

    slot = i % 2
    _tile_wait(y_hbm, ybuf.at[slot, 0], sem.at[slot])
    _tile_wait(y_hbm, ybuf.at[slot, 1], sem.at[slot])
    w12 = w12_ref[...]
    ffn = w12[:, 0:1] * ybuf[slot, 0] + w12[:, 1:2] * ybuf[slot, 1]
    res = _layer_norm_rows(alpha * x_ref[...] + ffn, g_ref[...], b_ref[...])
    if n_out == 1:
        out_refs[0][...] = res
    else:
        for (lo, hi), o_ref in zip(bounds, out_refs):
            @pl.when((i >= lo) & (i < hi))
            def _(o_ref=o_ref):
                o_ref[...] = res


def _combine(x, y_sorted, pos1, pos2, w12, g, b, *, alpha, splits):
    M, D = x.shape
    rows = _pick(math.gcd(*splits), SUB_ROWS, 16)
    nt = M // rows
    p1 = pos1.reshape(nt, 1, rows)
    p2 = pos2.reshape(nt, 1, rows)
    idx = pl.BlockSpec((1, 1, rows), lambda i: (i, 0, 0), memory_space=pltpu.SMEM)
    nidx = pl.BlockSpec((1, 1, rows), lambda i: (jnp.minimum(i + 1, nt - 1), 0, 0), memory_space=pltpu.SMEM)
    row = pl.BlockSpec((rows, D), lambda i: (i, 0))
    vec = pl.BlockSpec((1, D), lambda i: (0, 0))
    bounds, out_specs, out_shape = [], [], []
    lo = 0
    for s in splits:
        nb = s // rows
        bounds.append((lo, lo + nb))
        out_specs.append(pl.BlockSpec((rows, D), lambda i, lo=lo, nb=nb: (jnp.clip(i - lo, 0, nb - 1), 0)))
        out_shape.append(jax.ShapeDtypeStruct((s, D), F32))
        lo += nb
    return pl.pallas_call(
        functools.partial(_combine_kernel, rows=rows, alpha=alpha, bounds=tuple(bounds)),
        grid=(nt,),
        in_specs=[idx, idx, nidx, nidx, row, pl.BlockSpec((rows, LANES), lambda i: (i, 0)), vec, vec,
                  pl.BlockSpec(memory_space=pl.ANY)],
        out_specs=out_specs,
        out_shape=out_shape,
        scratch_shapes=[pltpu.VMEM((2, 2, rows, D), F32), pltpu.SemaphoreType.DMA((2,))],
        compiler_params=_cp(("arbitrary",)),
        name="moe_combine",
    )(p1, p2, p1, p2, x, w12, g.reshape(1, D), b.reshape(1, D), y_sorted)


def _piece_loop(nsub, big, fn):
    nbig = lax.div(nsub, jnp.int32(big))

    def big_body(p, c):
        fn(pl.multiple_of(p * (big * SUB_ROWS), big * SUB_ROWS), big * SUB_ROWS)
        return c

    lax.fori_loop(0, nbig, big_body, 0)

    def small_body(s, c):
        fn(pl.multiple_of(s * SUB_ROWS, SUB_ROWS), SUB_ROWS)
        return c

    lax.fori_loop(nbig * big, nsub, small_body, 0)


def _zero_tail(ref, nsub, n_sub):
    def fill(s, c):
        r0 = pl.multiple_of(s * SUB_ROWS, SUB_ROWS)
        ref[pl.ds(r0, SUB_ROWS), :] = jnp.zeros((SUB_ROWS, ref.shape[1]), ref.dtype)
        return c

    lax.fori_loop(nsub, n_sub, fill, 0)


def _moe_up_kernel(exp_ref, nsub_ref, x_ref, wg_ref, wu_ref, h_ref, *, n_sub, big):
    nsub = nsub_ref[pl.program_id(0)]

    def piece(r0, rows):
        x = x_ref[pl.ds(r0, rows), :]
        g = _dot(x, wg_ref[...].astype(BF16))
        u = _dot(x, wu_ref[...].astype(BF16))
        h_ref[pl.ds(r0, rows), :] = (g * jax.nn.sigmoid(g) * u).astype(BF16)

    _piece_loop(nsub, big, piece)
    _zero_tail(h_ref, nsub, n_sub)


def _moe_up(xs, wg, wu, n_used, ch_exp, ch_nsub, *, tm_chunk, tf):
    n_slot, D = xs.shape
    F = wg.shape[2]
    tf = _pick(F, tf, LANES)
    n_sub = tm_chunk // SUB_ROWS
    return pl.pallas_call(
        functools.partial(_moe_up_kernel, n_sub=n_sub, big=min(BIG_SUBS, n_sub)),
        grid_spec=pltpu.PrefetchScalarGridSpec(
            num_scalar_prefetch=2,
            grid=(n_used, F // tf),
            in_specs=[pl.BlockSpec((tm_chunk, D), lambda c, f, e, ns: (c, 0), pipeline_mode=pl.Buffered(1)),
                      pl.BlockSpec((None, D, tf), lambda c, f, e, ns: (e[c], 0, f)),
                      pl.BlockSpec((None, D, tf), lambda c, f, e, ns: (e[c], 0, f))],
            out_specs=pl.BlockSpec((tm_chunk, tf), lambda c, f, e, ns: (c, f)),
        ),
        out_shape=jax.ShapeDtypeStruct((n_slot, F), BF16),
        compiler_params=_cp(("arbitrary", "arbitrary")),
        name="moe_up",
    )(ch_exp, ch_nsub, xs, wg, wu)


def _moe_down_kernel(exp_ref, nsub_ref, h_ref, wd_ref, y_ref, *, n_sub, big):
    k = pl.program_id(2)
    nsub = nsub_ref[pl.program_id(0)]

    @pl.when(k == 0)
    def _():
        def first(r0, rows):
            y_ref[pl.ds(r0, rows), :] = _dot(h_ref[pl.ds(r0, rows), :], wd_ref[...].astype(BF16))

        _piece_loop(nsub, big, first)
        _zero_tail(y_ref, nsub, n_sub)

    @pl.when(k > 0)
    def _():
        def acc(r0, rows):
            y_ref[pl.ds(r0, rows), :] += _dot(h_ref[pl.ds(r0, rows), :], wd_ref[...].astype(BF16))

        _piece_loop(nsub, big, acc)


def _moe_down(hs, wd, n_used, ch_exp, ch_nsub, *, tm_chunk, tn, tk):
    n_slot, F = hs.shape
    D = wd.shape[2]
    tn = _pick(D, tn, LANES)
    tk = _pick(F, tk, LANES)
    n_sub = tm_chunk // SUB_ROWS
    return pl.pallas_call(
        functools.partial(_moe_down_kernel, n_sub=n_sub, big=min(BIG_SUBS, n_sub)),
        grid_spec=pltpu.PrefetchScalarGridSpec(
            num_scalar_prefetch=2,
            grid=(n_used, D // tn, F // tk),
            in_specs=[pl.BlockSpec((tm_chunk, tk), lambda c, n, k, e, ns: (c, k)),
                      pl.BlockSpec((None, tk, tn), lambda c, n, k, e, ns: (e[c], k, n))],
            out_specs=pl.BlockSpec((tm_chunk, tn), lambda c, n, k, e, ns: (c, n)),
        ),
        out_shape=jax.ShapeDtypeStruct((n_slot, D), F32),
        compiler_params=_cp(("arbitrary", "arbitrary", "arbitrary")),
        name="moe_down",
    )(ch_exp, ch_nsub, hs, wd)


def _chunk_tables(cnt, *, n_exp, tm_chunk, n_chunk):
    cnt = cnt.astype(jnp.int32)
    nch_e = (cnt + tm_chunk - 1) // tm_chunk
    cend = jnp.cumsum(nch_e)
    cstart = cend - nch_e
    total = cend[-1]
    j = jnp.arange(n_chunk, dtype=jnp.int32)
    jj = jnp.minimum(j, total - 1)
    e = jnp.minimum(jnp.sum((jj[:, None] >= cend[None, :]).astype(jnp.int32), axis=1), n_exp - 1)
    local = jj - cstart[e]
    rows = jnp.clip(cnt[e] - local * tm_chunk, 0, tm_chunk)
    nsub = jnp.where(j < total, (rows + SUB_ROWS - 1) // SUB_ROWS, 0).astype(jnp.int32)
    base = (cstart * tm_chunk).astype(F32)
    return total.astype(jnp.int32), e.astype(jnp.int32), nsub, base


def _moe_ffn(x_f32, w_router, wg, wu, wd, ln_g, ln_b, *, alpha, splits):
    M, D = x_f32.shape
    n_exp = wg.shape[0]
    avg = TOP_K * M / n_exp
    tm_chunk = SUB_ROWS * max(1, -(-int(avg * 1.11) // SUB_ROWS))
    n_chunk = n_exp + (TOP_K * M) // tm_chunk
    n_slot = n_chunk * tm_chunk

    wr_pad = jnp.pad(w_router, ((0, 0), (0, LANES - n_exp)))
    sel, e12, w12 = _router(x_f32, wr_pad, n_exp=n_exp)
    rank, cnt = _rank(sel)
    n_used, ch_exp, ch_nsub, base = _chunk_tables(cnt[0, :n_exp], n_exp=n_exp, tm_chunk=tm_chunk,
                                                  n_chunk=n_chunk)
    base_row = jnp.pad(base, (0, LANES - n_exp)).reshape(1, LANES)
    pos = _slots(rank, e12, base_row)
    pos1, pos2 = pos[:, 0], pos[:, 1]
    src = _invert(pos1, pos2, n_slot)
    sub_per_chunk = tm_chunk // SUB_ROWS
    tile_valid = (jnp.arange(sub_per_chunk, dtype=jnp.int32)[None, :] < ch_nsub[:, None])
    tile_valid = tile_valid.astype(jnp.int32).reshape(-1)
    xs = _dispatch(x_f32, src, tile_valid)
    hs = _moe_up(xs, wg, wu, n_used, ch_exp, ch_nsub, tm_chunk=tm_chunk, tf=256)
    ys = _moe_down(hs, wd, n_used, ch_exp, ch_nsub, tm_chunk=tm_chunk, tn=1024, tk=1024)
    return _combine(x_f32, ys, pos1, pos2, w12, ln_g, ln_b, alpha=alpha, splits=splits)


def _last_rows(x2d, batch, seq, n, col0, col1):
    return jnp.stack([lax.slice(x2d, (b * seq + seq - n, col0), ((b + 1) * seq, col1)) for b in range(batch)])


def kernel(x_prompt, x_sample, state_rglru_h, state_rglru_conv, state_pool, rg_w_in, rg_conv_w, rg_conv_b,
           rg_w_a, rg_b_a, rg_w_x, rg_b_x, rg_lambda, rg_w_out, pool_w, pool_b, pool_scale, ffn_w_gate,
           ffn_w_up, ffn_w_down, moe_router, moe_w_gate, moe_w_up, moe_w_down, ln_mix_g, ln_mix_b,
           ln_ffn_g, ln_ffn_b):
    B, S, D = x_prompt.shape
    Bs, Ss, _ = x_sample.shape
    depth = ln_mix_g.shape[0]
    alpha = float((2 * depth) ** 0.25)
    d_rnn = rg_w_a.shape[1] * rg_w_a.shape[2]
    gb = rg_w_a.shape[2]
    n_pool_groups = pool_w.shape[1]
    windows = tuple(2 ** (i + 1) for i in range(n_pool_groups))
    pool_buf = state_pool.shape[2]
    conv_w = rg_conv_w.shape[1]
    Mp, Ms = B * S, Bs * Ss

    xp2 = x_prompt.reshape(Mp, D)
    xs2 = jnp.swapaxes(x_sample, 0, 1).reshape(Ms, D)
    x_parts = [xp2, xs2]
    xb = _cast_rows(x_parts)

    new_h_p, new_conv_p, new_pool_p = [], [], []
    new_h_s, new_conv_s, new_pool_s = [], [], []
    y_parts = None
    for i in range(depth):
        j = i // 2
        last = i == depth - 1
        if i % 2 == 0:
            u = _mm_fullk([xb], [rg_w_in[j]], kind="gelu_split", out_dtype=F32, tm=1024, tn=512,
                          n_gelu_cols=d_rnn, name="rg_in_proj")
            wax = jnp.concatenate([rg_w_a[j], rg_w_x[j]], axis=-1).astype(BF16)
            vec = lambda v: v.reshape(1, d_rnn)
            args = (rg_conv_w[j], vec(rg_conv_b[j]), wax, vec(rg_b_a[j]), vec(rg_b_x[j]), vec(rg_lambda[j]))
            yg_p, h_p = _rglru_prompt(u, *args, batch=B, seq=S, d_rnn=d_rnn, gb=gb)
            cs3 = jnp.swapaxes(state_rglru_conv[j], 0, 1)
            yg_s, h_s = _rglru_sample(u, Mp, Ss, cs3, state_rglru_h[j], *args, gb=gb)
            new_h_p.append(h_p)
            new_h_s.append(h_s)
            new_conv_p.append(_last_rows(u, B, S, conv_w - 1, d_rnn, 2 * d_rnn))
            n_new = min(Ss, conv_w - 1)
            tail = lax.slice(u, (Mp + (Ss - n_new) * Bs, d_rnn), (Mp + Ms, 2 * d_rnn)).reshape(n_new, Bs, d_rnn)
            hist = tail if n_new == conv_w - 1 else jnp.concatenate([cs3[n_new:], tail], axis=0)
            new_conv_s.append(jnp.swapaxes(hist, 0, 1))
            mix = _mm_fullk([yg_p, yg_s.reshape(Ms, d_rnn)], [rg_w_out[j]], kind="plain", out_dtype=F32,
                            tm=1024, tn=512, name="rg_out_proj")
            x, xb = _deepnorm(x_parts, mix, ln_mix_g[i], ln_mix_b[i], alpha=alpha, want_bf16=True, name="ln_mix")
            hid = _mm_fullk([xb], [ffn_w_gate[j], ffn_w_up[j]], kind="swiglu", out_dtype=BF16, tm=768, tn=512,
                            name="ffn_up")
            ffn = _mm_ktiled(hid, ffn_w_down[j], tm=2304, tn=1024, tk=1024, name="ffn_down")
            x, xb = _deepnorm([x], ffn, ln_ffn_g[i], ln_ffn_b[i], alpha=alpha, want_bf16=False, name="ln_ffn")
            x_parts = [x]
        else:
            if len(x_parts) != 1:
                x = jnp.concatenate(x_parts, axis=0)
            pw = pool_w[j].astype(BF16)
            vec = lambda v: v.reshape(1, D)
            pargs = (pw, vec(pool_b[j]), vec(pool_scale[j]), vec(ln_mix_g[i]), vec(ln_mix_b[i]))
            x_s3 = lax.slice(x, (Mp, 0), (Mp + Ms, D)).reshape(Ss, Bs, D)
            st3 = jnp.swapaxes(state_pool[j], 0, 1)
            xm = _pool_prompt(x, *pargs, batch=B, seq=S, windows=windows, alpha=alpha)
            xs_f = _pool_sample(st3, x_s3, *pargs, windows=windows, pos0=PAST_LEN, alpha=alpha)
            new_pool_p.append(_last_rows(x, B, S, pool_buf, 0, D))
            new_pool_s.append(jnp.swapaxes(jnp.concatenate([st3, x_s3], axis=0)[-pool_buf:], 0, 1))
            xm = lax.dynamic_update_slice(xm, xs_f.reshape(Ms, D), (Mp, 0))
            outs = _moe_ffn(xm, moe_router[j], moe_w_gate[j], moe_w_up[j], moe_w_down[j],
                            ln_ffn_g[i], ln_ffn_b[i], alpha=alpha, splits=(Mp, Ms))
            x_parts = list(outs)
            if not last:
                xb = _cast_rows(x_parts)

    if len(x_parts) == 1:
        x_parts = [x_parts[0][:Mp], x_parts[0][Mp:]]
    y_prompt = x_parts[0].reshape(B, S, D)
    y_sample = jnp.swapaxes(x_parts[1].reshape(Ss, Bs, D), 0, 1)
    return (y_prompt, y_sample, jnp.stack(new_h_p), jnp.stack(new_conv_p), jnp.stack(new_pool_p),
            jnp.stack(new_h_s), jnp.stack(new_conv_s), jnp.stack(new_pool_s))
```
